```python
import math
import jax, jax.numpy as jnp
from jax import lax
import numpy as np

D_MODEL = 1024
BATCH = 16
SEQ = 2048
DEPTH = 1
DEC_BATCH = 128
DEC_SEQ = 4
PAST_LEN = 8192
PAGE_SIZE = 128

MIX_WIDTH = D_MODEL
GLA_HEADS = 4
GLA_DV = MIX_WIDTH // 2 // GLA_HEADS
GLA_DK = GLA_DV // 2
GLA_LR_RANK = 16
GATE_TAU = 16.0
GLA_CHUNK = 64
DSA_HEADS = 8
DSA_HEAD_DIM = (MIX_WIDTH - GLA_HEADS * GLA_DV) // DSA_HEADS
ROT_DIM = DSA_HEAD_DIM // 4
IDX_HEADS = 8
IDX_DIM = 64
IDX_ROT_DIM = IDX_DIM // 4
IDX_TOPK = 256
Q_BLOCK = 128
ROPE_THETA = 500000.0
D_FF = -(-8 * D_MODEL // (3 * 256)) * 256
DEEPNORM_ALPHA = (2 * DEPTH) ** 0.25
DEEPNORM_BETA = (8 * DEPTH) ** -0.25
NORM_EPS = 1e-5
SPLIT_SIZES = (GLA_HEADS * GLA_DK, GLA_HEADS * GLA_DK, GLA_HEADS * GLA_DV, GLA_HEADS * GLA_DV, GLA_LR_RANK,
               DSA_HEADS * DSA_HEAD_DIM, DSA_HEADS * DSA_HEAD_DIM, DSA_HEADS * DSA_HEAD_DIM,
               IDX_HEADS * IDX_DIM, IDX_DIM, IDX_HEADS)
IN_COLS = sum(SPLIT_SIZES)

kernel_name = 'gla_dsa_hybrid_decoder_step'


def layer_norm(x, g, b):
    xf = x.astype(jnp.float32)
    mu = jnp.mean(xf, -1, keepdims=True)
    var = jnp.mean(jnp.square(xf - mu), -1, keepdims=True)
    return ((xf - mu) * lax.rsqrt(var + NORM_EPS) * g + b).astype(x.dtype)


def rope(x, pos, rot):
    half = rot // 2
    inv = jnp.power(ROPE_THETA, -jnp.arange(half, dtype=jnp.float32) * (2.0 / rot))
    ang = pos[:, None] * inv[None, :]
    cos = jnp.cos(ang)[:, None, :]
    sin = jnp.sin(ang)[:, None, :]
    x1 = x[..., :half].astype(jnp.float32)
    x2 = x[..., half:rot].astype(jnp.float32)
    r = jnp.concatenate([x1 * cos - x2 * sin, x2 * cos + x1 * sin], axis=-1).astype(x.dtype)
    return jnp.concatenate([r, x[..., rot:]], axis=-1)


def project(x, pos, w_in, w_lr_up, b_gate):
    B, L, _ = x.shape
    u = x @ w_in
    cuts = np.cumsum(SPLIT_SIZES)[:-1].tolist()
    gq, gk, gv, gr, glr, dq, dk, dv, iq, ik, iw = jnp.split(u, cuts, axis=-1)
    gq = gq.reshape(B, L, GLA_HEADS, GLA_DK) * GLA_DK ** -0.5
    gk = gk.reshape(B, L, GLA_HEADS, GLA_DK)
    gv = gv.reshape(B, L, GLA_HEADS, GLA_DV)
    gg = (jax.nn.log_sigmoid((glr @ w_lr_up + b_gate).astype(jnp.float32)) / GATE_TAU).reshape(B, L, GLA_HEADS, GLA_DK)
    dq = rope(dq.reshape(B, L, DSA_HEADS, DSA_HEAD_DIM), pos, ROT_DIM)
    dk = rope(dk.reshape(B, L, DSA_HEADS, DSA_HEAD_DIM), pos, ROT_DIM)
    dv = dv.reshape(B, L, DSA_HEADS, DSA_HEAD_DIM)
    iq = rope(iq.reshape(B, L, IDX_HEADS, IDX_DIM), pos, IDX_ROT_DIM)
    ik = rope(ik[:, :, None, :], pos, IDX_ROT_DIM)[:, :, 0, :]
    iw = iw * IDX_HEADS ** -0.5
    return gq, gk, gv, gr, gg, dq, dk, dv, iq, ik, iw


def gla_chunked(q, k, v, g, s0):
    B, L, H, dk = q.shape
    dv = v.shape[-1]
    C = math.gcd(L, GLA_CHUNK)
    n = L // C

    def to_chunks(a):
        return jnp.moveaxis(a.astype(jnp.float32).reshape(B, n, C, *a.shape[2:]), 1, 0)

    tri = jnp.tril(jnp.ones((C, C), bool))[None, :, :, None, None]

    def step(S, inp):
        qc, kc, vc, gc = inp
        b = jnp.cumsum(gc, axis=1)
        o_inter = jnp.einsum('bthk,bhkv->bthv', qc * jnp.exp(b), S)
        diff = b[:, :, None] - b[:, None, :]
        decay = jnp.exp(jnp.where(tri, diff, -jnp.inf))
        A = jnp.einsum('bthk,bshk,btshk->bhts', qc, kc, decay)
        o_intra = jnp.einsum('bhts,bshv->bthv', A, vc)
        b_last = b[:, -1]
        S = jnp.exp(b_last)[..., None] * S + jnp.einsum('bshk,bshv->bhkv', kc * jnp.exp(b_last[:, None] - b), vc)
        return S, o_inter + o_intra

    S, o = lax.scan(step, s0.astype(jnp.float32), (to_chunks(q), to_chunks(k), to_chunks(v), to_chunks(g)))
    return jnp.moveaxis(o, 0, 1).reshape(B, L, H, dv), S


def index_select(qi, wi, ki, qpos, kpos, topk):
    s = jnp.einsum('...thd,...sd->...ths', qi.astype(jnp.float32), ki.astype(jnp.float32))
    score = jnp.einsum('...ths,...th->...ts', jax.nn.relu(s), wi.astype(jnp.float32)) * IDX_DIM ** -0.5
    mask = kpos[None, :] <= qpos[:, None]
    score = jnp.where(mask, score, -jnp.inf)
    _, idx = lax.top_k(score, topk)
    valid = idx <= qpos[:, None]
    return idx, valid


def sparse_attend(q, kg, vg, valid):
    logits = jnp.einsum('...thd,...tkhd->...thk', q.astype(jnp.float32), kg.astype(jnp.float32)) * DSA_HEAD_DIM ** -0.5
    logits = jnp.where(valid[..., None, :], logits, -jnp.inf)
    p = jax.nn.softmax(logits, axis=-1)
    return jnp.einsum('...thk,...tkhd->...thd', p.astype(vg.dtype), vg)


def dsa_prompt(q, k, v, qi, ki, wi):
    B, L, H, hd = q.shape
    topk = min(IDX_TOPK, L // 4)
    nb = L // Q_BLOCK
    kpos = jnp.arange(L)

    def one_block(bj):
        b = bj // nb
        t0 = (bj % nb) * Q_BLOCK
        qb = lax.dynamic_slice_in_dim(q[b], t0, Q_BLOCK, 0)
        qib = lax.dynamic_slice_in_dim(qi[b], t0, Q_BLOCK, 0)
        wib = lax.dynamic_slice_in_dim(wi[b], t0, Q_BLOCK, 0)
        qpos = t0 + jnp.arange(Q_BLOCK)
        idx, valid = index_select(qib, wib, ki[b], qpos, kpos, topk)
        kg = jnp.take(k[b], idx, axis=0)
        vg = jnp.take(v[b], idx, axis=0)
        return sparse_attend(qb, kg, vg, valid)

    out = lax.map(one_block, jnp.arange(B * nb))
    return out.reshape(B, L, H, hd)


def dsa_sample(q, k_new, v_new, qi, ki_new, wi, cache_k, cache_v, cache_kidx, layer, page_table):
    B, T, H, hd = q.shape
    past = page_table.shape[1] * PAGE_SIZE
    L = past + T
    topk = min(IDX_TOPK, L // 4)
    ki_past = cache_kidx[layer, page_table].reshape(B, past, IDX_DIM)
    ki_all = jnp.concatenate([ki_past, ki_new.astype(ki_past.dtype)], axis=1)
    qpos = past + jnp.arange(T)
    idx, valid = index_select(qi, wi, ki_all, qpos, jnp.arange(L), topk)
    bidx = jnp.arange(B)[:, None, None]
    is_past = (idx < past)[..., None, None]
    pi = jnp.minimum(idx, past - 1)
    phys = page_table[bidx, pi // PAGE_SIZE]
    off = pi % PAGE_SIZE
    ni = jnp.clip(idx - past, 0, T - 1)
    kg = jnp.where(is_past, cache_k[layer, phys, off], k_new[bidx, ni])
    vg = jnp.where(is_past, cache_v[layer, phys, off], v_new[bidx, ni])
    return sparse_attend(q, kg, vg, valid)


def merge(gla_o, gr, dsa_o, gla_norm_g, w_o):
    B, L = gr.shape[:2]
    of = gla_o.astype(jnp.float32)
    of = of * lax.rsqrt(jnp.mean(jnp.square(of), -1, keepdims=True) + NORM_EPS) * gla_norm_g
    gated = of.reshape(B, L, -1).astype(gr.dtype) * jax.nn.silu(gr)
    cat = jnp.concatenate([gated, dsa_o.reshape(B, L, -1).astype(gr.dtype)], axis=-1)
    return cat @ w_o


def post_block(x, mix_out, ln1_g, ln1_b, wg, wu, wd, ln2_g, ln2_b):
    h = layer_norm(DEEPNORM_ALPHA * x + mix_out, ln1_g, ln1_b)
    f = (jax.nn.silu(h @ wg) * (h @ wu)) @ wd
    return layer_norm(DEEPNORM_ALPHA * h + f, ln2_g, ln2_b)


def setup_inputs(seed: int = 0) -> dict:
    key = jax.random.key(seed)
    ks = jax.random.split(key, 24)
    f32 = jnp.float32
    n_pages = PAST_LEN // PAGE_SIZE
    n_used = DEC_BATCH * n_pages
    n_pool = n_used + max(1, n_used // 4)

    def nrm(k, shape, s):
        return jax.random.normal(k, shape, f32) * s

    x_prompt = nrm(ks[0], (BATCH, SEQ, D_MODEL), 1.0)
    x_sample = nrm(ks[1], (DEC_BATCH, DEC_SEQ, D_MODEL), 1.0)
    cache_k = nrm(ks[2], (DEPTH, n_pool, PAGE_SIZE, DSA_HEADS, DSA_HEAD_DIM), 1.0)
    cache_v = nrm(ks[3], (DEPTH, n_pool, PAGE_SIZE, DSA_HEADS, DSA_HEAD_DIM), 1.0)
    cache_kidx = nrm(ks[4], (DEPTH, n_pool, PAGE_SIZE, IDX_DIM), 1.0)
    state_gla = nrm(ks[5], (DEPTH, DEC_BATCH, GLA_HEADS, GLA_DK, GLA_DV), 1.0)
    page_table = jax.random.permutation(ks[6], n_pool)[:n_used].reshape(DEC_BATCH, n_pages).astype(jnp.int32)
    w_in = nrm(ks[7], (DEPTH, D_MODEL, IN_COLS), D_MODEL ** -0.5)
    w_lr_up = nrm(ks[8], (DEPTH, GLA_LR_RANK, GLA_HEADS * GLA_DK), GLA_LR_RANK ** -0.5)
    b_gate = 1.0 + nrm(ks[9], (DEPTH, GLA_HEADS * GLA_DK), 0.1)
    gla_norm_g = 1.0 + nrm(ks[10], (DEPTH, GLA_DV), 0.01)
    w_o = nrm(ks[11], (DEPTH, MIX_WIDTH, D_MODEL), MIX_WIDTH ** -0.5 * DEEPNORM_BETA)
    ln1_g = 1.0 + nrm(ks[12], (DEPTH, D_MODEL), 0.01)
    ln1_b = nrm(ks[13], (DEPTH, D_MODEL), 0.01)
    w_ffn_gate = nrm(ks[14], (DEPTH, D_MODEL, D_FF), D_MODEL ** -0.5)
    w_ffn_up = nrm(ks[15], (DEPTH, D_MODEL, D_FF), D_MODEL ** -0.5)
    w_ffn_down = nrm(ks[16], (DEPTH, D_FF, D_MODEL), D_FF ** -0.5 * DEEPNORM_BETA)
    ln2_g = 1.0 + nrm(ks[17], (DEPTH, D_MODEL), 0.01)
    ln2_b = nrm(ks[18], (DEPTH, D_MODEL), 0.01)
    return {'x_prompt': x_prompt, 'x_sample': x_sample, 'cache_k': cache_k, 'cache_v': cache_v,
            'cache_kidx': cache_kidx, 'state_gla': state_gla, 'page_table': page_table,
            'w_in': w_in, 'w_lr_up': w_lr_up, 'b_gate': b_gate, 'gla_norm_g': gla_norm_g, 'w_o': w_o,
            'ln1_g': ln1_g, 'ln1_b': ln1_b, 'w_ffn_gate': w_ffn_gate, 'w_ffn_up': w_ffn_up,
            'w_ffn_down': w_ffn_down, 'ln2_g': ln2_g, 'ln2_b': ln2_b}


def reference(x_prompt, x_sample, cache_k, cache_v, cache_kidx, state_gla, page_table,
              w_in, w_lr_up, b_gate, gla_norm_g, w_o, ln1_g, ln1_b,
              w_ffn_gate, w_ffn_up, w_ffn_down, ln2_g, ln2_b):
    Bp, Lp, _ = x_prompt.shape
    Bs, Ts, _ = x_sample.shape
    past = page_table.shape[1] * PAGE_SIZE
    pos_p = jnp.arange(Lp, dtype=jnp.float32)
    pos_s = past + jnp.arange(Ts, dtype=jnp.float32)
    xp, xs = x_prompt, x_sample
    kp_l, vp_l, kip_l, sp_l, ks_l, vs_l, kis_l, ss_l = [], [], [], [], [], [], [], []
    for l in range(DEPTH):
        gq, gk, gv, gr, gg, dq, dk, dv, iq, ik, iw = project(xp, pos_p, w_in[l], w_lr_up[l], b_gate[l])
        s0 = jnp.zeros((Bp, GLA_HEADS, GLA_DK, GLA_DV), jnp.float32)
        go, s_p = gla_chunked(gq, gk, gv, gg, s0)
        do = dsa_prompt(dq, dk, dv, iq, ik, iw)
        xp_next = post_block(xp, merge(go, gr, do, gla_norm_g[l], w_o[l]), ln1_g[l], ln1_b[l],
                             w_ffn_gate[l], w_ffn_up[l], w_ffn_down[l], ln2_g[l], ln2_b[l])
        kp_l.append(dk); vp_l.append(dv); kip_l.append(ik); sp_l.append(s_p.astype(state_gla.dtype))
        xp = xp_next
        gq, gk, gv, gr, gg, dq, dk, dv, iq, ik, iw = project(xs, pos_s, w_in[l], w_lr_up[l], b_gate[l])
        go, s_s = gla_chunked(gq, gk, gv, gg, state_gla[l])
        do = dsa_sample(dq, dk, dv, iq, ik, iw, cache_k, cache_v, cache_kidx, l, page_table)
        xs_next = post_block(xs, merge(go, gr, do, gla_norm_g[l], w_o[l]), ln1_g[l], ln1_b[l],
                             w_ffn_gate[l], w_ffn_up[l], w_ffn_down[l], ln2_g[l], ln2_b[l])
        ks_l.append(dk); vs_l.append(dv); kis_l.append(ik); ss_l.append(s_s.astype(state_gla.dtype))
        xs = xs_next
    return (xp, xs,
            jnp.stack(kp_l), jnp.stack(vp_l), jnp.stack(kip_l), jnp.stack(sp_l),
            jnp.stack(ks_l), jnp.stack(vs_l), jnp.stack(kis_l), jnp.stack(ss_l))
```

```python
import functools

import numpy as np
import jax
import jax.numpy as jnp
from jax import lax
from jax.experimental import pallas as pl
from jax.experimental.pallas import tpu as pltpu

F32 = jnp.float32
BF16 = jnp.bfloat16
I32 = jnp.int32

GLA_HEADS = 4
GLA_DK = 64
GLA_DV = 128
GLA_LR_RANK = 16
GATE_TAU = 16.0
GLA_CHUNK = 64
DSA_HEADS = 8
DSA_HEAD_DIM = 64
ROT_DIM = 16
IDX_HEADS = 8
IDX_DIM = 64
IDX_TOPK = 256
PAGE_SIZE = 128
ROPE_THETA = 500000.0
NORM_EPS = 1e-5

LANES = 128
INT_MIN = int(np.iinfo(np.int32).min)
INT_MAX = int(np.iinfo(np.int32).max)
VMEM_LIMIT = 56 * 1024 * 1024

GQ_W = GLA_HEADS * GLA_DK
GV_W = GLA_HEADS * GLA_DV
DSA_W = DSA_HEADS * DSA_HEAD_DIM
IDX_W = IDX_HEADS * IDX_DIM
OFF_GQ = 0
OFF_GK = OFF_GQ + GQ_W
OFF_GV = OFF_GK + GQ_W
OFF_GR = OFF_GV + GV_W
OFF_DQ = OFF_GR + GV_W
OFF_DK = OFF_DQ + DSA_W
OFF_DV = OFF_DK + DSA_W
OFF_IQ = OFF_DV + DSA_W
OFF_MISC = OFF_IQ + IDX_W
N_COLS = OFF_MISC + LANES
MISC_GLR = IDX_DIM
MISC_IW = IDX_DIM + GLA_LR_RANK

NT_DIMS = (((1,), (1,)), ((), ()))
TN_DIMS = (((0,), (0,)), ((), ()))


def _dot(a, b):
    return jnp.dot(a, b, preferred_element_type=F32)


def _dot_nt(a, b):
    return lax.dot_general(a, b, NT_DIMS, preferred_element_type=F32)


def _dot_tn(a, b):
    return lax.dot_general(a, b, TN_DIMS, preferred_element_type=F32)


def _cparams(sem):
    return pltpu.CompilerParams(dimension_semantics=sem, vmem_limit_bytes=VMEM_LIMIT)


def _const_spec(shape):
    nd = len(shape)
    return pl.BlockSpec(shape, lambda *_: (0,) * nd, pipeline_mode=pl.Buffered(1))


def _proj_kernel(x_ref, w_ref, wlr_ref, bg_ref, cos_ref, sina_ref, sinb_ref,
                 gq_ref, gk_ref, gv_ref, gr_ref, gg_ref, dq_ref, dk_ref, dkb_ref,
                 dv_ref, dvb_ref, iq_ref, ik_ref, ikb_ref, iw_ref):
    xb = x_ref[...].astype(BF16)
    cosp = cos_ref[...]
    sina = sina_ref[...]
    sinb = sinb_ref[...]

    def seg(off, width):
        return _dot(xb, w_ref[:, off:off + width])

    def rope(u):
        return u * cosp + pltpu.roll(u, LANES - ROT_DIM // 2, 1) * sina + pltpu.roll(u, ROT_DIM // 2, 1) * sinb

    gq_ref[...] = seg(OFF_GQ, GQ_W) * (GLA_DK ** -0.5)
    gk_ref[...] = seg(OFF_GK, GQ_W)
    gv_ref[...] = seg(OFF_GV, GV_W).astype(BF16)
    gr_ref[...] = seg(OFF_GR, GV_W).astype(BF16)
    for i in range(DSA_W // LANES):
        sl = slice(i * LANES, (i + 1) * LANES)
        dq_ref[:, sl] = (rope(seg(OFF_DQ + i * LANES, LANES)) * (DSA_HEAD_DIM ** -0.5)).astype(BF16)
        k = rope(seg(OFF_DK + i * LANES, LANES))
        dk_ref[:, sl] = k
        dkb_ref[:, sl] = k.astype(BF16)
        iq_ref[:, sl] = (rope(seg(OFF_IQ + i * LANES, LANES)) * (IDX_DIM ** -0.5)).astype(BF16)
    v = seg(OFF_DV, DSA_W)
    dv_ref[...] = v
    dvb_ref[...] = v.astype(BF16)
    misc = seg(OFF_MISC, LANES)
    mr = rope(misc)
    ik_ref[...] = mr[:, :IDX_DIM]
    lane = lax.broadcasted_iota(I32, (1, LANES), 1)
    ikb_ref[...] = jnp.where(lane < IDX_DIM, mr, pltpu.roll(mr, IDX_DIM, 1)).astype(BF16)
    iw_ref[...] = misc[:, MISC_IW:MISC_IW + IDX_HEADS] * (IDX_HEADS ** -0.5)
    z = _dot(misc.astype(BF16), wlr_ref[...]) + bg_ref[...]
    gg_ref[...] = (jnp.minimum(z, 0.0) - jnp.log1p(jnp.exp(-jnp.abs(z)))) * (1.0 / GATE_TAU)


def _project(x2d, w, wlr, bg, cos_t, sina_t, sinb_t, tm):
    m, d = x2d.shape
    period = cos_t.shape[0]
    assert m % tm == 0 and period % tm == 0
    nper = period // tm
    row = lambda i: (i, 0)
    tab = lambda i: (i % nper, 0)
    outs = [
        (GQ_W, F32), (GQ_W, F32), (GV_W, BF16), (GV_W, BF16), (GQ_W, F32),
        (DSA_W, BF16), (DSA_W, F32), (DSA_W, BF16), (DSA_W, F32), (DSA_W, BF16),
        (IDX_W, BF16), (IDX_DIM, F32), (LANES, BF16), (IDX_HEADS, F32),
    ]
    return pl.pallas_call(
        _proj_kernel,
        grid=(m // tm,),
        in_specs=[
            pl.BlockSpec((tm, d), row),
            _const_spec((d, N_COLS)),
            _const_spec((LANES, GQ_W)),
            _const_spec((1, GQ_W)),
            pl.BlockSpec((tm, LANES), tab),
            pl.BlockSpec((tm, LANES), tab),
            pl.BlockSpec((tm, LANES), tab),
        ],
        out_specs=[pl.BlockSpec((tm, wd), row) for wd, _ in outs],
        out_shape=[jax.ShapeDtypeStruct((m, wd), dt) for wd, dt in outs],
        compiler_params=_cparams(("parallel",)),
        name="project",
    )(x2d, w, wlr, bg, cos_t, sina_t, sinb_t)


def _prep_w_in(w_in):
    sizes = (GQ_W, GQ_W, GV_W, GV_W, GLA_LR_RANK, DSA_W, DSA_W, DSA_W, IDX_W, IDX_DIM, IDX_HEADS)
    cuts = np.cumsum(sizes)[:-1].tolist()
    gq, gk, gv, gr, glr, dq, dk, dv, iq, ik, iw = jnp.split(w_in, cuts, axis=1)
    pad = jnp.zeros((w_in.shape[0], LANES - IDX_DIM - GLA_LR_RANK - IDX_HEADS), w_in.dtype)
    return jnp.concatenate([gq, gk, gv, gr, dq, dk, dv, iq, ik, glr, iw, pad], axis=1).astype(BF16)


def _prep_w_lr(w_lr_up):
    z0 = jnp.zeros((MISC_GLR, GQ_W), w_lr_up.dtype)
    z1 = jnp.zeros((LANES - MISC_GLR - GLA_LR_RANK, GQ_W), w_lr_up.dtype)
    return jnp.concatenate([z0, w_lr_up, z1], axis=0).astype(BF16)


def _rope_tables(pos):
    half = ROT_DIM // 2
    inv = jnp.power(ROPE_THETA, -jnp.arange(half, dtype=F32) * (2.0 / ROT_DIM))
    ang = pos[:, None] * inv[None, :]
    cos = jnp.cos(ang)
    sin = jnp.sin(ang)
    n = pos.shape[0]
    rest = DSA_HEAD_DIM - ROT_DIM
    z8 = jnp.zeros((n, half), F32)
    zr = jnp.zeros((n, rest), F32)
    cos_h = jnp.concatenate([cos, cos, jnp.ones((n, rest), F32)], axis=1)
    sina_h = jnp.concatenate([-sin, z8, zr], axis=1)
    sinb_h = jnp.concatenate([z8, sin, zr], axis=1)
    rep = LANES // DSA_HEAD_DIM
    return jnp.tile(cos_h, (1, rep)), jnp.tile(sina_h, (1, rep)), jnp.tile(sinb_h, (1, rep))


def _gla_kernel(gq_ref, gk_ref, gg_ref, gv_ref, gr_ref, s0_ref, gn_ref, o_ref, sout_ref, st_scr,
                *, chunk, n_sub):
    j = pl.program_id(1)
    lane = lax.broadcasted_iota(I32, (1, LANES), 1)
    lo = lane < GLA_DK

    @pl.when(j == 0)
    def _():
        zero = jnp.zeros((GLA_DK, GLA_DV), F32)
        for h in range(GLA_HEADS):
            s = s0_ref[0, h]
            full = jnp.concatenate([s, zero] if h % 2 == 0 else [zero, s], axis=0)
            st_scr[h] = full.T

    r = lax.broadcasted_iota(I32, (chunk, chunk), 0)
    c = lax.broadcasted_iota(I32, (chunk, chunk), 1)
    tri = r >= c
    trib = jnp.where(tri, 1.0, 0.0).astype(BF16)
    gn = gn_ref[...]
    for ci in range(n_sub):
        rows = slice(ci * chunk, (ci + 1) * chunk)
        g = gg_ref[0, rows, :]
        g_hi = g.astype(BF16)
        g_lo = (g - g_hi.astype(F32)).astype(BF16)
        b = _dot(trib, g_hi) + _dot(trib, g_lo)
        b_last = b[chunk - 1:chunk, :]
        q = gq_ref[0, rows, :] * jnp.exp(b)
        k = gk_ref[0, rows, :]
        k_in = k * jnp.exp(-b)
        k_out = k * jnp.exp(b_last - b)
        d_last = jnp.exp(b_last)
        for h in range(GLA_HEADS):
            p, hh = divmod(h, 2)
            sl = slice(p * LANES, (p + 1) * LANES)
            msk = lo if hh == 0 else jnp.logical_not(lo)
            qm = jnp.where(msk, q[:, sl], 0.0).astype(BF16)
            a = _dot_nt(qm, k_in[:, sl].astype(BF16))
            a = jnp.where(tri, a, 0.0).astype(BF16)
            vh = gv_ref[0, rows, h * GLA_DV:(h + 1) * GLA_DV]
            st = st_scr[h]
            o = _dot(a, vh) + _dot_nt(qm, st.astype(BF16))
            km = jnp.where(msk, k_out[:, sl], 0.0).astype(BF16)
            st_scr[h] = st * d_last[:, sl] + _dot_tn(vh, km)
            ms = jnp.mean(o * o, axis=1, keepdims=True)
            of = o * lax.rsqrt(ms + NORM_EPS) * gn
            gr = gr_ref[0, rows, h * GLA_DV:(h + 1) * GLA_DV].astype(F32)
            o_ref[0, rows, h * GLA_DV:(h + 1) * GLA_DV] = (of * (gr * jax.nn.sigmoid(gr))).astype(BF16)

    @pl.when(j == pl.num_programs(1) - 1)
    def _():
        for h in range(GLA_HEADS):
            off = (h % 2) * GLA_DK
            sout_ref[0, h] = st_scr[h].T[off:off + GLA_DK, :]


def _gla(gq, gk, gg, gv, gr, s0, gn, chunk, n_sub):
    bsz, length, _ = gq.shape
    tl = chunk * n_sub
    assert length % tl == 0
    tok = lambda b, j: (b, j, 0)
    st = lambda b, j: (b, 0, 0, 0)
    return pl.pallas_call(
        functools.partial(_gla_kernel, chunk=chunk, n_sub=n_sub),
        grid=(bsz, length // tl),
        in_specs=[
            pl.BlockSpec((1, tl, GQ_W), tok),
            pl.BlockSpec((1, tl, GQ_W), tok),
            pl.BlockSpec((1, tl, GQ_W), tok),
            pl.BlockSpec((1, tl, GV_W), tok),
            pl.BlockSpec((1, tl, GV_W), tok),
            pl.BlockSpec((1, GLA_HEADS, GLA_DK, GLA_DV), st),
            pl.BlockSpec((1, GLA_DV), lambda b, j: (0, 0)),
        ],
        out_specs=[
            pl.BlockSpec((1, tl, GV_W), tok),
            pl.BlockSpec((1, GLA_HEADS, GLA_DK, GLA_DV), st),
        ],
        out_shape=[
            jax.ShapeDtypeStruct((bsz, length, GV_W), BF16),
            jax.ShapeDtypeStruct((bsz, GLA_HEADS, GLA_DK, GLA_DV), F32),
        ],
        scratch_shapes=[pltpu.VMEM((GLA_HEADS, GLA_DV, LANES), F32)],
        compiler_params=_cparams(("parallel", "arbitrary")),
        name="gla",
    )(gq, gk, gg, gv, gr, s0, gn)


def _sort_key(score, causal):
    sc = jnp.where(score == 0.0, 0.0, score)
    bits = pltpu.bitcast(sc, I32)
    key = bits ^ ((bits >> 31) & INT_MAX)
    return jnp.where(causal, key, INT_MIN)


def _count(mask):
    return jnp.sum(jnp.where(mask, 1.0, 0.0), axis=1, keepdims=True)


def _topk_threshold(key_ref, kpos, topk, pcut_ref, idx_bits):
    rows = key_ref.shape[0]
    kf = float(topk)
    cnt0 = _count(key_ref[...] >= 0)
    thr0 = jnp.where(cnt0 >= kf, 0, INT_MIN).astype(I32)

    def body(i, thr):
        cand = thr + jnp.left_shift(jnp.int32(1), 30 - i)
        cnt = _count(key_ref[...] >= cand)
        return jnp.where(cnt >= kf, cand, thr)

    thr = lax.fori_loop(0, 31, body, thr0)
    key = key_ref[...]
    eq = key == thr
    need = kf - _count(key > thr)
    cut = jnp.logical_and(thr > INT_MIN, _count(eq) > need)
    pcut_ref[...] = jnp.full((rows, 1), INT_MAX, I32)

    @pl.when(jnp.max(jnp.where(cut, 1.0, 0.0)) > 0.0)
    def _():
        def pbody(i, pos):
            cand = pos + jnp.left_shift(jnp.int32(1), idx_bits - 1 - i)
            cnt = _count(jnp.logical_and(key_ref[...] == thr, kpos < cand))
            return jnp.where(cnt < need, cand, pos)

        pos = lax.fori_loop(0, idx_bits, pbody, jnp.zeros((rows, 1), I32))
        pcut_ref[...] = jnp.where(cut, pos, INT_MAX)

    return thr, pcut_ref[...]


def _select(key, thr, pcut, kpos, causal):
    keep = jnp.logical_or(key > thr, jnp.logical_and(key == thr, kpos <= pcut))
    return jnp.logical_and(keep, causal)


def _dsa_kernel(dq_ref, k_ref, v_ref, iq_ref, ik_ref, iw_ref, o_ref, key_scr, bias_scr, pcut_scr,
                *, tq, lk, qb0, topk):
    j = pl.program_id(1)
    t0 = (qb0 + j) * tq
    lane = lax.broadcasted_iota(I32, (1, LANES), 1)
    lo = lane < DSA_HEAD_DIM
    ik = ik_ref[0]
    iw = iw_ref[0]
    score = jnp.zeros((tq, lk), F32)
    for h in range(IDX_HEADS):
        p, hh = divmod(h, 2)
        qs = iq_ref[0, :, p * LANES:(p + 1) * LANES]
        qm = jnp.where(lo if hh == 0 else jnp.logical_not(lo), qs, jnp.zeros_like(qs))
        s = _dot_nt(qm, ik)
        score = score + jnp.maximum(s, 0.0) * iw[:, h:h + 1]
    kpos = lax.broadcasted_iota(I32, (tq, lk), 1)
    qpos = t0 + lax.broadcasted_iota(I32, (tq, lk), 0)
    causal = kpos <= qpos
    key_scr[...] = _sort_key(score, causal)
    thr, pcut = _topk_threshold(key_scr, kpos, topk, pcut_scr, int(lk - 1).bit_length())
    sel = _select(key_scr[...], thr, pcut, kpos, causal)
    bias_scr[...] = jnp.where(sel, 0.0, -jnp.inf)
    for p in range(DSA_HEADS // 2):
        sl = slice(p * LANES, (p + 1) * LANES)
        qs = dq_ref[0, :, sl]
        ks = k_ref[0, :, sl]
        vs = v_ref[0, :, sl]
        outs = []
        for hh in range(2):
            qm = jnp.where(lo if hh == 0 else jnp.logical_not(lo), qs, jnp.zeros_like(qs))
            logits = _dot_nt(qm, ks) + bias_scr[...]
            m = jnp.max(logits, axis=1, keepdims=True)
            e = jnp.exp(logits - m)
            l = jnp.sum(e, axis=1, keepdims=True)
            outs.append(_dot(e.astype(BF16), vs) / l)
        o_ref[0, :, sl] = jnp.where(lo, outs[0], outs[1]).astype(BF16)


def _dsa_prompt_bucket(dq, dkb, dvb, iq, ikb, iw, tq, qb0, nqb, lk, topk):
    bsz = dq.shape[0]
    qblk = lambda b, j: (b, qb0 + j, 0)
    kblk = lambda b, j: (b, 0, 0)
    return pl.pallas_call(
        functools.partial(_dsa_kernel, tq=tq, lk=lk, qb0=qb0, topk=topk),
        grid=(bsz, nqb),
        in_specs=[
            pl.BlockSpec((1, tq, DSA_W), qblk),
            pl.BlockSpec((1, lk, DSA_W), kblk),
            pl.BlockSpec((1, lk, DSA_W), kblk),
            pl.BlockSpec((1, tq, IDX_W), qblk),
            pl.BlockSpec((1, lk, LANES), kblk),
            pl.BlockSpec((1, tq, IDX_HEADS), qblk),
        ],
        out_specs=pl.BlockSpec((1, tq, DSA_W), lambda b, j: (b, j, 0)),
        out_shape=jax.ShapeDtypeStruct((bsz, nqb * tq, DSA_W), BF16),
        scratch_shapes=[
            pltpu.VMEM((tq, lk), I32),
            pltpu.VMEM((tq, lk), F32),
            pltpu.VMEM((tq, 1), I32),
        ],
        compiler_params=_cparams(("parallel", "arbitrary")),
        name="dsa_prompt",
    )(dq, dkb, dvb, iq, ikb, iw)


def _dsa_prompt(dq, dkb, dvb, iq, ikb, iw, tq, per_bucket):
    length = dq.shape[1]
    topk = min(IDX_TOPK, length // 4)
    nq = length // tq
    outs = []
    for qb0 in range(0, nq, per_bucket):
        lk = (qb0 + per_bucket) * tq
        outs.append(_dsa_prompt_bucket(dq, dkb, dvb, iq, ikb, iw, tq, qb0, per_bucket, lk, topk))
    return jnp.concatenate(outs, axis=1)


def _idx_sample_kernel(pt_ref, q_ref, w_ref, iknew_ref, *rest, n_pg):
    pages = rest[:n_pg]
    score_ref, snew_ref = rest[n_pg:]
    q = q_ref[0]
    w = w_ref[0]
    n_t = score_ref.shape[1]

    def head_sum(s):
        r = jnp.maximum(s, 0.0) * w
        return jnp.sum(r.reshape(n_t, IDX_HEADS, r.shape[1]), axis=1)

    kb = jnp.concatenate([pg[0] for pg in pages], axis=1).astype(BF16)
    score_ref[0] = head_sum(_dot(q, kb))

    @pl.when(pl.program_id(1) == 0)
    def _():
        snew_ref[0] = head_sum(_dot_nt(q, iknew_ref[0]))


def _idx_sample(page_table_flat, q_rows, w_rows, ik_new, kidx_pages, n_t, n_pages, n_pg):
    bsz = q_rows.shape[0]
    ngrp = n_pages // n_pg
    per_b = lambda b, g, pt: (b, 0, 0)

    def page_spec(i):
        return pl.BlockSpec((1, IDX_DIM, PAGE_SIZE),
                            lambda b, g, pt: (pt[b * n_pages + g * n_pg + i], 0, 0))

    grid_spec = pltpu.PrefetchScalarGridSpec(
        num_scalar_prefetch=1,
        grid=(bsz, ngrp),
        in_specs=[
            pl.BlockSpec((1, n_t * IDX_HEADS, IDX_DIM), per_b),
            pl.BlockSpec((1, n_t * IDX_HEADS, 1), per_b),
            pl.BlockSpec((1, LANES, IDX_DIM), per_b),
        ] + [page_spec(i) for i in range(n_pg)],
        out_specs=[
            pl.BlockSpec((1, n_t, n_pg * PAGE_SIZE), lambda b, g, pt: (b, 0, g)),
            pl.BlockSpec((1, n_t, LANES), per_b),
        ],
    )
    return pl.pallas_call(
        functools.partial(_idx_sample_kernel, n_pg=n_pg),
        grid_spec=grid_spec,
        out_shape=[
            jax.ShapeDtypeStruct((bsz, n_t, n_pages * PAGE_SIZE), F32),
            jax.ShapeDtypeStruct((bsz, n_t, LANES), F32),
        ],
        compiler_params=_cparams(("parallel", "arbitrary")),
        name="idx_sample",
    )(page_table_flat, q_rows, w_rows, ik_new, *([kidx_pages] * n_pg))


def _sample_positions(rows, n_t, past, width, row0):
    t = (row0 + lax.broadcasted_iota(I32, (rows, width), 0)) % n_t
    return past + t


def _thr_sample_kernel(score_ref, snew_ref, thr_ref, pcut_ref, key_scr, pcut_scr, *, n_t, past, topk):
    rows, width = key_scr.shape
    row0 = pl.program_id(0) * rows
    kpos = lax.broadcasted_iota(I32, (rows, width), 1)
    qpos = _sample_positions(rows, n_t, past, width, row0)
    causal = kpos <= qpos
    key_scr[:, :past] = _sort_key(score_ref[...], causal[:, :past])
    key_scr[:, past:] = _sort_key(snew_ref[...], causal[:, past:])
    thr, pcut = _topk_threshold(key_scr, kpos, topk, pcut_scr, int(width - 1).bit_length())
    thr_ref[...] = thr
    pcut_ref[...] = pcut


def _thr_sample(score2d, snew2d, n_t, past, topk, rb):
    m = score2d.shape[0]
    width = past + LANES
    row = lambda i: (i, 0)
    return pl.pallas_call(
        functools.partial(_thr_sample_kernel, n_t=n_t, past=past, topk=topk),
        grid=(m // rb,),
        in_specs=[pl.BlockSpec((rb, past), row), pl.BlockSpec((rb, LANES), row)],
        out_specs=[pl.BlockSpec((rb, 1), row), pl.BlockSpec((rb, 1), row)],
        out_shape=[jax.ShapeDtypeStruct((m, 1), I32), jax.ShapeDtypeStruct((m, 1), I32)],
        scratch_shapes=[pltpu.VMEM((rb, width), I32), pltpu.VMEM((rb, 1), I32)],
        compiler_params=_cparams(("parallel",)),
        name="thr_sample",
    )(score2d, snew2d)


def _attn_sample_kernel(pt_ref, q_ref, score_ref, snew_ref, thr_ref, pcut_ref, knew_ref, vnew_ref, *rest,
                        n_pg, n_t, past, n_new):
    kpages = rest[:n_pg]
    vpages = rest[n_pg:2 * n_pg]
    o_ref, m_scr, l_scr, acc_scr = rest[2 * n_pg:]
    g = pl.program_id(1)
    rows = n_t * DSA_HEADS
    width = n_pg * PAGE_SIZE

    @pl.when(g == 0)
    def _():
        m_scr[...] = jnp.full(m_scr.shape, -jnp.inf, F32)
        l_scr[...] = jnp.zeros(l_scr.shape, F32)
        acc_scr[...] = jnp.zeros(acc_scr.shape, F32)

    thr = thr_ref[0]
    pcut = pcut_ref[0]
    qpos = past + lax.broadcasted_iota(I32, (n_t, 1), 0)

    def per_head(a):
        return jnp.broadcast_to(a[:, None, :], (n_t, DSA_HEADS, a.shape[1])).reshape(rows, a.shape[1])

    def bias_of(score, kpos):
        causal = kpos <= qpos
        sel = _select(_sort_key(score, causal), thr, pcut, kpos, causal)
        return per_head(jnp.where(sel, 0.0, -jnp.inf))

    head_of_row = lax.broadcasted_iota(I32, (rows, DSA_W), 0) % DSA_HEADS
    head_of_col = lax.broadcasted_iota(I32, (rows, DSA_W), 1) // DSA_HEAD_DIM
    diag = head_of_row == head_of_col
    qbd = jnp.where(diag, per_head(q_ref[0]), 0.0).astype(BF16)

    def update(logits, pv):
        m_old = m_scr[...]
        m_new = jnp.maximum(m_old, jnp.max(logits, axis=1, keepdims=True))
        m_safe = jnp.where(m_new == -jnp.inf, 0.0, m_new)
        e = jnp.exp(logits - m_safe)
        alpha = jnp.exp(m_old - m_safe)
        l_scr[...] = alpha * l_scr[...] + jnp.sum(e, axis=1, keepdims=True)
        acc_scr[...] = alpha * acc_scr[...] + pv(e.astype(BF16))
        m_scr[...] = m_new

    kt = jnp.concatenate([pg[0].reshape(DSA_W, PAGE_SIZE) for pg in kpages], axis=1).astype(BF16)
    vt = jnp.concatenate([pg[0].reshape(DSA_W, PAGE_SIZE) for pg in vpages], axis=1).astype(BF16)
    kpos = g * width + lax.broadcasted_iota(I32, (n_t, width), 1)
    update(_dot(qbd, kt) + bias_of(score_ref[0], kpos), lambda e: _dot_nt(e, vt))

    @pl.when(g == pl.num_programs(1) - 1)
    def _():
        kpos_new = past + lax.broadcasted_iota(I32, (n_t, n_new), 1)
        bias_new = bias_of(snew_ref[0][:, :n_new], kpos_new)
        update(_dot_nt(qbd, knew_ref[0]) + bias_new, lambda e: _dot(e, vnew_ref[0]))
        o = jnp.where(diag, acc_scr[...] / l_scr[...], 0.0)
        o_ref[0] = jnp.sum(o.reshape(n_t, DSA_HEADS, DSA_W), axis=1)


def _attn_sample(page_table_flat, q, score, snew, thr, pcut, knew, vnew, k_pages, v_pages,
                 n_t, n_pages, n_pg, past):
    bsz = q.shape[0]
    ngrp = n_pages // n_pg
    n_new = knew.shape[1]
    rows = n_t * DSA_HEADS
    per_b = lambda b, g, pt: (b, 0, 0)

    def page_spec(i):
        return pl.BlockSpec((1, DSA_HEADS, DSA_HEAD_DIM, PAGE_SIZE),
                            lambda b, g, pt: (pt[b * n_pages + g * n_pg + i], 0, 0, 0))

    grid_spec = pltpu.PrefetchScalarGridSpec(
        num_scalar_prefetch=1,
        grid=(bsz, ngrp),
        in_specs=[
            pl.BlockSpec((1, n_t, DSA_W), per_b),
            pl.BlockSpec((1, n_t, n_pg * PAGE_SIZE), lambda b, g, pt: (b, 0, g)),
            pl.BlockSpec((1, n_t, LANES), per_b),
            pl.BlockSpec((1, n_t, 1), per_b),
            pl.BlockSpec((1, n_t, 1), per_b),
            pl.BlockSpec((1, n_new, DSA_W), per_b),
            pl.BlockSpec((1, n_new, DSA_W), per_b),
        ] + [page_spec(i) for i in range(n_pg)] * 2,
        out_specs=pl.BlockSpec((1, n_t, DSA_W), per_b),
        scratch_shapes=[
            pltpu.VMEM((rows, 1), F32),
            pltpu.VMEM((rows, 1), F32),
            pltpu.VMEM((rows, DSA_W), F32),
        ],
    )
    return pl.pallas_call(
        functools.partial(_attn_sample_kernel, n_pg=n_pg, n_t=n_t, past=past, n_new=n_new),
        grid_spec=grid_spec,
        out_shape=jax.ShapeDtypeStruct((bsz, n_t, DSA_W), F32),
        compiler_params=_cparams(("parallel", "arbitrary")),
        name="attn_sample",
    )(page_table_flat, q, score, snew, thr, pcut, knew, vnew,
      *([k_pages] * n_pg), *([v_pages] * n_pg))


def _layer_norm(x, g, b):
    mu = jnp.mean(x, axis=-1, keepdims=True)
    xc = x - mu
    var = jnp.mean(xc * xc, axis=-1, keepdims=True)
    return xc * lax.rsqrt(var + NORM_EPS) * g + b


def _post_kernel(x_ref, ga_ref, ds_ref, wo_ref, l1g_ref, l1b_ref, wg_ref, wu_ref, wd_ref,
                 l2g_ref, l2b_ref, y_ref, *, alpha):
    mix = _dot(ga_ref[...], wo_ref[:GV_W, :]) + _dot(ds_ref[...], wo_ref[GV_W:, :])
    h = _layer_norm(alpha * x_ref[...] + mix, l1g_ref[...], l1b_ref[...])
    hb = h.astype(BF16)
    a = _dot(hb, wg_ref[...])
    u = _dot(hb, wu_ref[...])
    f = _dot((a * jax.nn.sigmoid(a) * u).astype(BF16), wd_ref[...])
    y_ref[...] = _layer_norm(alpha * h + f, l2g_ref[...], l2b_ref[...])


def _post(x2d, gated, dsa, wo, l1g, l1b, wg, wu, wd, l2g, l2b, alpha, tm):
    m, d = x2d.shape
    dff = wg.shape[1]
    tm = min(tm, m)
    assert m % tm == 0
    row = lambda i: (i, 0)
    return pl.pallas_call(
        functools.partial(_post_kernel, alpha=alpha),
        grid=(m // tm,),
        in_specs=[
            pl.BlockSpec((tm, d), row),
            pl.BlockSpec((tm, GV_W), row),
            pl.BlockSpec((tm, DSA_W), row),
            _const_spec((GV_W + DSA_W, d)),
            _const_spec((1, d)),
            _const_spec((1, d)),
            _const_spec((d, dff)),
            _const_spec((d, dff)),
            _const_spec((dff, d)),
            _const_spec((1, d)),
            _const_spec((1, d)),
        ],
        out_specs=pl.BlockSpec((tm, d), row),
        out_shape=jax.ShapeDtypeStruct((m, d), F32),
        compiler_params=_cparams(("parallel",)),
        name="post",
    )(x2d, gated, dsa, wo, l1g, l1b, wg, wu, wd, l2g, l2b)


def _layer_weights(l, w_in, w_lr_up, b_gate, gla_norm_g, w_o, ln1_g, ln1_b,
                   w_ffn_gate, w_ffn_up, w_ffn_down, ln2_g, ln2_b):
    row = lambda a: a[l][None, :].astype(F32)
    return dict(
        w=_prep_w_in(w_in[l]), wlr=_prep_w_lr(w_lr_up[l]), bg=row(b_gate), gn=row(gla_norm_g),
        wo=w_o[l].astype(BF16), l1g=row(ln1_g), l1b=row(ln1_b),
        wg=w_ffn_gate[l].astype(BF16), wu=w_ffn_up[l].astype(BF16), wd=w_ffn_down[l].astype(BF16),
        l2g=row(ln2_g), l2b=row(ln2_b))


def _prompt_layer(x, wts, alpha):
    bsz, length, d = x.shape
    tm = 512 if length % 512 == 0 else length
    tabs = _rope_tables(jnp.arange(length, dtype=F32))
    x2d = x.reshape(bsz * length, d)
    (gq, gk, gv, gr, gg, dq, dk, dkb, dv, dvb, iq, ik, ikb, iw) = _project(
        x2d, wts["w"], wts["wlr"], wts["bg"], *tabs, tm)
    r3 = lambda a: a.reshape(bsz, length, a.shape[-1])
    chunk = int(np.gcd(length, GLA_CHUNK))
    n_sub = 4 if length % (4 * chunk) == 0 else 1
    s0 = jnp.zeros((bsz, GLA_HEADS, GLA_DK, GLA_DV), F32)
    gated, state = _gla(r3(gq), r3(gk), r3(gg), r3(gv), r3(gr), s0, wts["gn"], chunk, n_sub)
    tq = 128
    per_bucket = 4 if (length // tq) % 4 == 0 else 1
    dsa = _dsa_prompt(r3(dq), r3(dkb), r3(dvb), r3(iq), r3(ikb), r3(iw), tq, per_bucket)
    y = _post(x2d, gated.reshape(bsz * length, GV_W), dsa.reshape(bsz * length, DSA_W),
              wts["wo"], wts["l1g"], wts["l1b"], wts["wg"], wts["wu"], wts["wd"],
              wts["l2g"], wts["l2b"], alpha, 256)
    return (y.reshape(bsz, length, d),
            dk.reshape(bsz, length, DSA_HEADS, DSA_HEAD_DIM),
            dv.reshape(bsz, length, DSA_HEADS, DSA_HEAD_DIM),
            ik.reshape(bsz, length, IDX_DIM), state)


def _sample_layer(x, wts, alpha, cache_k, cache_v, cache_kidx, state, page_table):
    bsz, n_t, d = x.shape
    n_pages = page_table.shape[1]
    past = n_pages * PAGE_SIZE
    m = bsz * n_t
    pos = past + jnp.arange(n_t, dtype=F32)
    tabs = [jnp.tile(t, (bsz, 1)) for t in _rope_tables(pos)]
    x2d = x.reshape(m, d)
    (gq, gk, gv, gr, gg, dq, dk, dkb, dv, dvb, iq, ik, ikb, iw) = _project(
        x2d, wts["w"], wts["wlr"], wts["bg"], *tabs, m)
    chunk = 16
    pad3 = lambda a: jnp.pad(a.reshape(bsz, n_t, a.shape[-1]), ((0, 0), (0, chunk - n_t), (0, 0)))
    gated, state_new = _gla(pad3(gq), pad3(gk), pad3(gg), pad3(gv), pad3(gr), state, wts["gn"], chunk, 1)
    gated = gated[:, :n_t].reshape(m, GV_W)
    topk = min(IDX_TOPK, (past + n_t) // 4)
    n_pg = 8
    pt_flat = page_table.reshape(-1)
    q_rows = iq.reshape(bsz, n_t * IDX_HEADS, IDX_DIM)
    w_rows = iw.reshape(bsz, n_t * IDX_HEADS, 1)
    ik_new = jnp.pad(ikb[:, :IDX_DIM].reshape(bsz, n_t, IDX_DIM), ((0, 0), (0, LANES - n_t), (0, 0)))
    score, snew = _idx_sample(pt_flat, q_rows, w_rows, ik_new, cache_kidx, n_t, n_pages, n_pg)
    thr, pcut = _thr_sample(score.reshape(m, past), snew.reshape(m, LANES), n_t, past, topk, 32)
    n_new = 16
    new3 = lambda a: jnp.pad(a.reshape(bsz, n_t, DSA_W), ((0, 0), (0, n_new - n_t), (0, 0)))
    dsa = _attn_sample(pt_flat, dq.astype(F32).reshape(bsz, n_t, DSA_W), score, snew,
                       thr.reshape(bsz, n_t, 1), pcut.reshape(bsz, n_t, 1), new3(dkb), new3(dvb),
                       cache_k, cache_v, n_t, n_pages, n_pg, past)
    y = _post(x2d, gated, dsa.reshape(m, DSA_W).astype(BF16),
              wts["wo"], wts["l1g"], wts["l1b"], wts["wg"], wts["wu"], wts["wd"],
              wts["l2g"], wts["l2b"], alpha, 256)
    return (y.reshape(bsz, n_t, d),
            dk.reshape(bsz, n_t, DSA_HEADS, DSA_HEAD_DIM),
            dv.reshape(bsz, n_t, DSA_HEADS, DSA_HEAD_DIM),
            ik.reshape(bsz, n_t, IDX_DIM), state_new)


def kernel(x_prompt, x_sample, cache_k, cache_v, cache_kidx, state_gla, page_table, w_in, w_lr_up, b_gate, gla_norm_g, w_o, ln1_g, ln1_b, w_ffn_gate, w_ffn_up, w_ffn_down, ln2_g, ln2_b):
    depth = w_in.shape[0]
    alpha = float((2 * depth) ** 0.25)
    n_pool = cache_k.shape[1]
    xp, xs = x_prompt, x_sample
    outs_p = [[] for _ in range(4)]
    outs_s = [[] for _ in range(4)]
    for l in range(depth):
        wts = _layer_weights(l, w_in, w_lr_up, b_gate, gla_norm_g, w_o, ln1_g, ln1_b,
                             w_ffn_gate, w_ffn_up, w_ffn_down, ln2_g, ln2_b)
        xp, *rest_p = _prompt_layer(xp, wts, alpha)
        feat = lambda c: jnp.transpose(c, (0, 1, 3, 4, 2)).reshape(
            depth * n_pool, DSA_HEADS, DSA_HEAD_DIM, PAGE_SIZE)
        xs, *rest_s = _sample_layer(
            xs, wts, alpha, feat(cache_k), feat(cache_v),
            jnp.transpose(cache_kidx, (0, 1, 3, 2)).reshape(depth * n_pool, IDX_DIM, PAGE_SIZE),
            state_gla[l], page_table + l * n_pool)
        for acc, val in zip(outs_p, rest_p):
            acc.append(val)
        for acc, val in zip(outs_s, rest_s):
            acc.append(val)
    stack = lambda seq: jnp.stack(seq)
    return (xp, xs, *[stack(a) for a in outs_p], *[stack(a) for a in outs_s])
```

```python
import functools

import numpy as np
import jax
import jax.numpy as jnp
from jax import lax
from jax.experimental import pallas as pl
from jax.experimental.pallas import tpu as pltpu

F32 = jnp.float32
BF16 = jnp.bfloat16
I32 = jnp.int32

GLA_HEADS = 4
GLA_DK = 64
GLA_DV = 128
GLA_LR_RANK = 16
GATE_TAU = 16.0
GLA_CHUNK = 64
DSA_HEADS = 8
DSA_HEAD_DIM = 64
ROT_DIM = 16
IDX_HEADS = 8
IDX_DIM = 64
IDX_TOPK = 256
PAGE_SIZE = 128
ROPE_THETA = 500000.0
NORM_EPS = 1e-5

LANES = 128
SUBLANES = 8
ROW_CHUNK = 64
INT_MIN = int(np.iinfo(np.int32).min)
INT_MAX = int(np.iinfo(np.int32).max)
VMEM_LIMIT = 56 * 1024 * 1024

GQ_W = GLA_HEADS * GLA_DK
GV_W = GLA_HEADS * GLA_DV
DSA_W = DSA_HEADS * DSA_HEAD_DIM
IDX_W = IDX_HEADS * IDX_DIM
OFF_GQ = 0
OFF_GK = OFF_GQ + GQ_W
OFF_GV = OFF_GK + GQ_W
OFF_GR = OFF_GV + GV_W
OFF_DQ = OFF_GR + GV_W
OFF_DK = OFF_DQ + DSA_W
OFF_DV = OFF_DK + DSA_W
OFF_IQ = OFF_DV + DSA_W
OFF_MISC = OFF_IQ + IDX_W
N_COLS = OFF_MISC + LANES
MISC_GLR = IDX_DIM
MISC_IW = IDX_DIM + GLA_LR_RANK

NT_DIMS = (((1,), (1,)), ((), ()))
TN_DIMS = (((0,), (0,)), ((), ()))


def _dot(a, b):
    return jnp.dot(a, b, preferred_element_type=F32)


def _dot_nt(a, b):
    return lax.dot_general(a, b, NT_DIMS, preferred_element_type=F32)


def _dot_tn(a, b):
    return lax.dot_general(a, b, TN_DIMS, preferred_element_type=F32)


def _cparams(sem):
    return pltpu.CompilerParams(dimension_semantics=sem, vmem_limit_bytes=VMEM_LIMIT)


def _const_spec(shape):
    nd = len(shape)
    return pl.BlockSpec(shape, lambda *_: (0,) * nd, pipeline_mode=pl.Buffered(1))


def _proj_shared(x_ref, w_ref, wlr_ref, bg_ref, cos_ref, sina_ref, sinb_ref,
                 gq_ref, gk_ref, gv_ref, gr_ref, gg_ref, dq_ref, dkb_ref, iq_ref, ikb_ref):
    xb = x_ref[...].astype(BF16)
    cosp = cos_ref[...]
    sina = sina_ref[...]
    sinb = sinb_ref[...]

    def seg(off, width):
        return _dot(xb, w_ref[:, off:off + width])

    def rope(u):
        return u * cosp + pltpu.roll(u, LANES - ROT_DIM // 2, 1) * sina + pltpu.roll(u, ROT_DIM // 2, 1) * sinb

    gq_ref[...] = seg(OFF_GQ, GQ_W) * (GLA_DK ** -0.5)
    gk_ref[...] = seg(OFF_GK, GQ_W)
    gv_ref[...] = seg(OFF_GV, GV_W).astype(BF16)
    gr_ref[...] = seg(OFF_GR, GV_W).astype(BF16)
    keys = []
    for i in range(DSA_W // LANES):
        sl = slice(i * LANES, (i + 1) * LANES)
        dq_ref[:, sl] = (rope(seg(OFF_DQ + i * LANES, LANES)) * (DSA_HEAD_DIM ** -0.5)).astype(BF16)
        k = rope(seg(OFF_DK + i * LANES, LANES))
        dkb_ref[:, sl] = k.astype(BF16)
        keys.append(k)
        iq_ref[:, sl] = (rope(seg(OFF_IQ + i * LANES, LANES)) * (IDX_DIM ** -0.5)).astype(BF16)
    misc = seg(OFF_MISC, LANES)
    mr = rope(misc)
    lane = lax.broadcasted_iota(I32, (1, LANES), 1)
    ikb_ref[...] = jnp.where(lane < IDX_DIM, mr, pltpu.roll(mr, IDX_DIM, 1)).astype(BF16)
    z = _dot(misc.astype(BF16), wlr_ref[...]) + bg_ref[...]
    gg_ref[...] = (jnp.minimum(z, 0.0) - jnp.log1p(jnp.exp(-jnp.abs(z)))) * (1.0 / GATE_TAU)
    return xb, keys, misc, mr


def _proj_kernel(x_ref, w_ref, wlr_ref, bg_ref, cos_ref, sina_ref, sinb_ref,
                 gq_ref, gk_ref, gv_ref, gr_ref, gg_ref, dq_ref, dkb_ref, iq_ref, ikb_ref,
                 dk_ref, dv_ref, dvb_ref, ik_ref, iw_ref):
    xb, keys, misc, mr = _proj_shared(x_ref, w_ref, wlr_ref, bg_ref, cos_ref, sina_ref, sinb_ref,
                                      gq_ref, gk_ref, gv_ref, gr_ref, gg_ref, dq_ref, dkb_ref, iq_ref, ikb_ref)
    for i, k in enumerate(keys):
        dk_ref[:, i * LANES:(i + 1) * LANES] = k
    v = _dot(xb, w_ref[:, OFF_DV:OFF_DV + DSA_W])
    dv_ref[...] = v
    dvb_ref[...] = v.astype(BF16)
    ik_ref[...] = mr[:, :IDX_DIM]
    iw_ref[...] = misc[:, MISC_IW:MISC_IW + IDX_HEADS] * (IDX_HEADS ** -0.5)


def _proj_fm_kernel(x_ref, w_ref, wvt_ref, wlr_ref, bg_ref, cos_ref, sina_ref, sinb_ref,
                    gq_ref, gk_ref, gv_ref, gr_ref, gg_ref, dq_ref, dkb_ref, iq_ref, ikb_ref,
                    kt_ref, vt_ref, vtb_ref, ikt_ref, iwt_ref):
    xb, keys, misc, mr = _proj_shared(x_ref, w_ref, wlr_ref, bg_ref, cos_ref, sina_ref, sinb_ref,
                                      gq_ref, gk_ref, gv_ref, gr_ref, gg_ref, dq_ref, dkb_ref, iq_ref, ikb_ref)
    for i, k in enumerate(keys):
        kt_ref[0, i * LANES:(i + 1) * LANES, :] = k.T
    vt = _dot_nt(wvt_ref[...], xb)
    vt_ref[0] = vt
    vtb_ref[0] = vt.astype(BF16)
    mt = mr.T
    ikt_ref[0] = mt[:IDX_DIM]
    iwt_ref[0] = mt[MISC_IW:MISC_IW + IDX_HEADS] * (IDX_HEADS ** -0.5)


_PROJ_SHARED_OUTS = [
    (GQ_W, F32), (GQ_W, F32), (GV_W, BF16), (GV_W, BF16), (GQ_W, F32),
    (DSA_W, BF16), (DSA_W, BF16), (IDX_W, BF16), (LANES, BF16),
]


def _project(x2d, w, wlr, bg, cos_t, sina_t, sinb_t, tm):
    m, d = x2d.shape
    period = cos_t.shape[0]
    assert m % tm == 0 and period % tm == 0
    nper = period // tm
    row = lambda i: (i, 0)
    tab = lambda i: (i % nper, 0)
    outs = _PROJ_SHARED_OUTS + [(DSA_W, F32), (DSA_W, F32), (DSA_W, BF16), (IDX_DIM, F32), (IDX_HEADS, F32)]
    return pl.pallas_call(
        _proj_kernel,
        grid=(m // tm,),
        in_specs=[
            pl.BlockSpec((tm, d), row),
            _const_spec((d, N_COLS)),
            _const_spec((LANES, GQ_W)),
            _const_spec((1, GQ_W)),
            pl.BlockSpec((tm, LANES), tab),
            pl.BlockSpec((tm, LANES), tab),
            pl.BlockSpec((tm, LANES), tab),
        ],
        out_specs=[pl.BlockSpec((tm, wd), row) for wd, _ in outs],
        out_shape=[jax.ShapeDtypeStruct((m, wd), dt) for wd, dt in outs],
        compiler_params=_cparams(("parallel",)),
        name="project",
    )(x2d, w, wlr, bg, cos_t, sina_t, sinb_t)


def _project_fm(x2d, w, wvt, wlr, bg, cos_t, sina_t, sinb_t, tm, bsz):
    m, d = x2d.shape
    length = m // bsz
    assert length % tm == 0 and cos_t.shape[0] == length
    nper = length // tm
    row = lambda i: (i, 0)
    tab = lambda i: (i % nper, 0)
    fm = lambda i: (i // nper, 0, i % nper)
    fm_outs = [(DSA_W, F32), (DSA_W, F32), (DSA_W, BF16), (IDX_DIM, F32), (IDX_HEADS, F32)]
    return pl.pallas_call(
        _proj_fm_kernel,
        grid=(m // tm,),
        in_specs=[
            pl.BlockSpec((tm, d), row),
            _const_spec((d, N_COLS)),
            _const_spec((DSA_W, d)),
            _const_spec((LANES, GQ_W)),
            _const_spec((1, GQ_W)),
            pl.BlockSpec((tm, LANES), tab),
            pl.BlockSpec((tm, LANES), tab),
            pl.BlockSpec((tm, LANES), tab),
        ],
        out_specs=[pl.BlockSpec((tm, wd), row) for wd, _ in _PROJ_SHARED_OUTS]
        + [pl.BlockSpec((1, wd, tm), fm) for wd, _ in fm_outs],
        out_shape=[jax.ShapeDtypeStruct((m, wd), dt) for wd, dt in _PROJ_SHARED_OUTS]
        + [jax.ShapeDtypeStruct((bsz, wd, length), dt) for wd, dt in fm_outs],
        compiler_params=_cparams(("parallel",)),
        name="project_fm",
    )(x2d, w, wvt, wlr, bg, cos_t, sina_t, sinb_t)


_IN_SIZES = (GQ_W, GQ_W, GV_W, GV_W, GLA_LR_RANK, DSA_W, DSA_W, DSA_W, IDX_W, IDX_DIM, IDX_HEADS)


def _prep_w_in(w_in):
    cuts = np.cumsum(_IN_SIZES)[:-1].tolist()
    gq, gk, gv, gr, glr, dq, dk, dv, iq, ik, iw = jnp.split(w_in, cuts, axis=1)
    pad = jnp.zeros((w_in.shape[0], LANES - IDX_DIM - GLA_LR_RANK - IDX_HEADS), w_in.dtype)
    w = jnp.concatenate([gq, gk, gv, gr, dq, dk, dv, iq, ik, glr, iw, pad], axis=1).astype(BF16)
    return w, jnp.transpose(dv).astype(BF16)


def _prep_w_lr(w_lr_up):
    z0 = jnp.zeros((MISC_GLR, GQ_W), w_lr_up.dtype)
    z1 = jnp.zeros((LANES - MISC_GLR - GLA_LR_RANK, GQ_W), w_lr_up.dtype)
    return jnp.concatenate([z0, w_lr_up, z1], axis=0).astype(BF16)


def _rope_tables(pos):
    half = ROT_DIM // 2
    inv = jnp.power(ROPE_THETA, -jnp.arange(half, dtype=F32) * (2.0 / ROT_DIM))
    ang = pos[:, None] * inv[None, :]
    cos = jnp.cos(ang)
    sin = jnp.sin(ang)
    n = pos.shape[0]
    rest = DSA_HEAD_DIM - ROT_DIM
    z8 = jnp.zeros((n, half), F32)
    zr = jnp.zeros((n, rest), F32)
    cos_h = jnp.concatenate([cos, cos, jnp.ones((n, rest), F32)], axis=1)
    sina_h = jnp.concatenate([-sin, z8, zr], axis=1)
    sinb_h = jnp.concatenate([z8, sin, zr], axis=1)
    rep = LANES // DSA_HEAD_DIM
    return jnp.tile(cos_h, (1, rep)), jnp.tile(sina_h, (1, rep)), jnp.tile(sinb_h, (1, rep))


def _gla_kernel(gq_ref, gk_ref, gg_ref, gv_ref, gr_ref, s0_ref, gn_ref, o_ref, sout_ref, st_scr,
                *, chunk, n_sub, n_seq):
    j = pl.program_id(1)
    lane = lax.broadcasted_iota(I32, (1, LANES), 1)
    lo = lane < GLA_DK

    @pl.when(j == 0)
    def _():
        zero = jnp.zeros((GLA_DK, GLA_DV), F32)
        for s in range(n_seq):
            for h in range(GLA_HEADS):
                s0 = s0_ref[s, h]
                full = jnp.concatenate([s0, zero] if h % 2 == 0 else [zero, s0], axis=0)
                st_scr[s * GLA_HEADS + h] = full.T

    r = lax.broadcasted_iota(I32, (chunk, chunk), 0)
    c = lax.broadcasted_iota(I32, (chunk, chunk), 1)
    tri = r >= c
    trib = jnp.where(tri, 1.0, 0.0).astype(BF16)
    gn = gn_ref[...]
    for s in range(n_seq):
        for ci in range(n_sub):
            rows = slice(ci * chunk, (ci + 1) * chunk)
            g = gg_ref[s, rows, :]
            g_hi = g.astype(BF16)
            g_lo = (g - g_hi.astype(F32)).astype(BF16)
            b = _dot(trib, g_hi) + _dot(trib, g_lo)
            b_last = b[chunk - 1:chunk, :]
            q = gq_ref[s, rows, :] * jnp.exp(b)
            k = gk_ref[s, rows, :]
            k_in = k * jnp.exp(-b)
            k_out = k * jnp.exp(b_last - b)
            d_last = jnp.exp(b_last)
            for h in range(GLA_HEADS):
                p, hh = divmod(h, 2)
                sl = slice(p * LANES, (p + 1) * LANES)
                vsl = slice(h * GLA_DV, (h + 1) * GLA_DV)
                msk = lo if hh == 0 else jnp.logical_not(lo)
                qm = jnp.where(msk, q[:, sl], 0.0).astype(BF16)
                a = _dot_nt(qm, k_in[:, sl].astype(BF16))
                a = jnp.where(tri, a, 0.0).astype(BF16)
                vh = gv_ref[s, rows, vsl]
                st = st_scr[s * GLA_HEADS + h]
                o = _dot(a, vh) + _dot_nt(qm, st.astype(BF16))
                km = jnp.where(msk, k_out[:, sl], 0.0).astype(BF16)
                st_scr[s * GLA_HEADS + h] = st * d_last[:, sl] + _dot_tn(vh, km)
                ms = jnp.mean(o * o, axis=1, keepdims=True)
                of = o * lax.rsqrt(ms + NORM_EPS) * gn
                gr = gr_ref[s, rows, vsl].astype(F32)
                o_ref[s, rows, vsl] = (of * (gr * jax.nn.sigmoid(gr))).astype(BF16)

    @pl.when(j == pl.num_programs(1) - 1)
    def _():
        for s in range(n_seq):
            for h in range(GLA_HEADS):
                off = (h % 2) * GLA_DK
                sout_ref[s, h] = st_scr[s * GLA_HEADS + h].T[off:off + GLA_DK, :]


def _gla(gq, gk, gg, gv, gr, s0, gn, chunk, n_sub, n_seq):
    bsz, length, _ = gq.shape
    tl = chunk * n_sub
    assert length % tl == 0 and bsz % n_seq == 0
    tok = lambda b, j: (b, j, 0)
    st = lambda b, j: (b, 0, 0, 0)
    return pl.pallas_call(
        functools.partial(_gla_kernel, chunk=chunk, n_sub=n_sub, n_seq=n_seq),
        grid=(bsz // n_seq, length // tl),
        in_specs=[
            pl.BlockSpec((n_seq, tl, GQ_W), tok),
            pl.BlockSpec((n_seq, tl, GQ_W), tok),
            pl.BlockSpec((n_seq, tl, GQ_W), tok),
            pl.BlockSpec((n_seq, tl, GV_W), tok),
            pl.BlockSpec((n_seq, tl, GV_W), tok),
            pl.BlockSpec((n_seq, GLA_HEADS, GLA_DK, GLA_DV), st),
            pl.BlockSpec((1, GLA_DV), lambda b, j: (0, 0)),
        ],
        out_specs=[
            pl.BlockSpec((n_seq, tl, GV_W), tok),
            pl.BlockSpec((n_seq, GLA_HEADS, GLA_DK, GLA_DV), st),
        ],
        out_shape=[
            jax.ShapeDtypeStruct((bsz, length, GV_W), BF16),
            jax.ShapeDtypeStruct((bsz, GLA_HEADS, GLA_DK, GLA_DV), F32),
        ],
        scratch_shapes=[pltpu.VMEM((n_seq * GLA_HEADS, GLA_DV, LANES), F32)],
        compiler_params=_cparams(("parallel", "arbitrary")),
        name="gla",
    )(gq, gk, gg, gv, gr, s0, gn)


def _sort_key(score, causal):
    sc = jnp.where(score == 0.0, 0.0, score)
    bits = pltpu.bitcast(sc, I32)
    key = bits ^ ((bits >> 31) & INT_MAX)
    return jnp.where(causal, key, INT_MIN)


def _topk_threshold(count, shape, topk, pcut_ref, idx_bits):
    kf = float(topk)
    cnt0 = count(lambda key, kpos: key >= 0)
    thr0 = jnp.where(cnt0 >= kf, 0, INT_MIN).astype(I32)

    def body(i, thr):
        cand = thr + jnp.left_shift(jnp.int32(1), 30 - i)
        cnt = count(lambda key, kpos: key >= cand)
        return jnp.where(cnt >= kf, cand, thr)

    thr = lax.fori_loop(0, 31, body, thr0)
    need = kf - count(lambda key, kpos: key > thr)
    n_eq = count(lambda key, kpos: key == thr)
    cut = jnp.logical_and(thr > INT_MIN, n_eq > need)
    pcut_ref[...] = jnp.full(shape, INT_MAX, I32)

    @pl.when(jnp.max(jnp.where(cut, 1.0, 0.0)) > 0.0)
    def _():
        def pbody(i, pos):
            cand = pos + jnp.left_shift(jnp.int32(1), idx_bits - 1 - i)
            cnt = count(lambda key, kpos: jnp.logical_and(key == thr, kpos < cand))
            return jnp.where(cnt < need, cand, pos)

        pos = lax.fori_loop(0, idx_bits, pbody, jnp.zeros(shape, I32))
        pcut_ref[...] = jnp.where(cut, pos, INT_MAX)

    return thr, pcut_ref[...]


def _select(key, thr, pcut, kpos, causal):
    keep = jnp.logical_or(key > thr, jnp.logical_and(key == thr, kpos <= pcut))
    return jnp.logical_and(keep, causal)


def _dsa_kernel(dq_ref, iq_ref, iwt_ref, k_ref, vt_ref, ik_ref, o_ref,
                key_scr, bias_scr, logit_scr, p_scr, out_scr, pcut_scr, *, tq, kt, nkt, qb0, topk, idx_bits):
    j = pl.program_id(1)
    t0 = (qb0 + j) * tq
    lane = lax.broadcasted_iota(I32, (1, LANES), 1)
    lo = lane < DSA_HEAD_DIM
    qpos = t0 + lax.broadcasted_iota(I32, (1, tq), 1)
    wt = iwt_ref[0]

    def head_pair(ref, p):
        qs = ref[0, :, p * LANES:(p + 1) * LANES]
        zero = jnp.zeros_like(qs)
        return jnp.concatenate([jnp.where(lo, qs, zero), jnp.where(lo, zero, qs)], axis=0)

    def rows_of(c):
        return pl.ds(pl.multiple_of(c * kt, kt), kt)

    def kpos_of(c):
        return c * kt + lax.broadcasted_iota(I32, (kt, tq), 0)

    def col_sum(x):
        return jnp.sum(x.reshape(kt // SUBLANES, SUBLANES, tq), axis=0)

    n_pair = IDX_HEADS // 2
    iq2 = [head_pair(iq_ref, p) for p in range(n_pair)]
    w2 = [jnp.concatenate([wt[2 * p:2 * p + 1, :], wt[2 * p + 1:2 * p + 2, :]], axis=1) for p in range(n_pair)]

    def score_tile(c, carry):
        rs = pl.ds(pl.multiple_of(c * tq, tq), tq)
        ik_t = ik_ref[0, rs, :]
        acc2 = jnp.zeros((tq, 2 * tq), F32)
        for p in range(n_pair):
            acc2 = acc2 + jnp.maximum(_dot_nt(ik_t, iq2[p]), 0.0) * w2[p]
        kpos = c * tq + lax.broadcasted_iota(I32, (tq, tq), 0)
        key_scr[rs, :] = _sort_key(acc2[:, :tq] + acc2[:, tq:], kpos <= qpos)
        return carry

    lax.fori_loop(0, nkt * kt // tq, score_tile, 0, unroll=2)

    def count(pred):
        acc = jnp.zeros((SUBLANES, tq), F32)
        for c in range(nkt):
            hit = pred(key_scr[c * kt:(c + 1) * kt, :], kpos_of(c))
            acc = acc + col_sum(jnp.where(hit, 1.0, 0.0))
        return jnp.sum(acc, axis=0, keepdims=True)

    thr, pcut = _topk_threshold(count, (1, tq), topk, pcut_scr, idx_bits)

    def bias_tile(c, carry):
        kpos = kpos_of(c)
        causal = kpos <= qpos
        sel = _select(key_scr[rows_of(c), :], thr, pcut, kpos, causal)
        bias_scr[rows_of(c), :] = jnp.where(sel, 0.0, -jnp.inf)
        return carry

    lax.fori_loop(0, nkt, bias_tile, 0)

    extent = nkt * kt
    n_chunk = extent // ROW_CHUNK
    n_pair = DSA_HEADS // 2
    q2s = [head_pair(dq_ref, p) for p in range(n_pair)]

    def chunk_rows(i):
        return pl.ds(pl.multiple_of(i * ROW_CHUNK, ROW_CHUNK), ROW_CHUNK)

    def col_max(x):
        return jnp.max(x.reshape(kt // SUBLANES, SUBLANES, tq), axis=0)

    def logit_body(c, carry):
        rs = rows_of(c)
        b = bias_scr[rs, :]
        out = []
        for p in range(n_pair):
            x = _dot_nt(k_ref[0, rs, p * LANES:(p + 1) * LANES], q2s[p])
            xa = x[:, :tq] + b
            xb = x[:, tq:] + b
            logit_scr[rs, 2 * p * tq:(2 * p + 2) * tq] = jnp.concatenate([xa, xb], axis=1)
            out.append(jnp.maximum(carry[2 * p], col_max(xa)))
            out.append(jnp.maximum(carry[2 * p + 1], col_max(xb)))
        return tuple(out)

    neg = jnp.full((SUBLANES, tq), -jnp.inf, F32)
    m8 = lax.fori_loop(0, nkt, logit_body, (neg,) * DSA_HEADS)
    m_all = jnp.concatenate([jnp.max(m, axis=0, keepdims=True) for m in m8], axis=1)

    def exp_body(i, l8):
        e = jnp.exp(logit_scr[chunk_rows(i), :] - m_all)
        p_scr[chunk_rows(i), :] = e.astype(BF16)
        return l8 + jnp.sum(e.reshape(ROW_CHUNK // SUBLANES, SUBLANES, DSA_HEADS * tq), axis=0)

    l8 = lax.fori_loop(0, n_chunk, exp_body, jnp.zeros((SUBLANES, DSA_HEADS * tq), F32), unroll=2)
    l_all = jnp.sum(l8, axis=0, keepdims=True)
    for h in range(DSA_HEADS):
        vsl = slice(h * DSA_HEAD_DIM, (h + 1) * DSA_HEAD_DIM)
        qsl = slice(h * tq, (h + 1) * tq)
        out_scr[vsl, :] = _dot(vt_ref[0, vsl, :], p_scr[:, qsl]) / l_all[:, qsl]

    o_ref[0] = out_scr[...].T.astype(BF16)


def _dsa_prompt_bucket(dq, iq, iwt, dkb, vtb, ikb, tq, kt, qb0, nqb, topk, idx_bits):
    bsz = dq.shape[0]
    extent = (qb0 + nqb) * tq
    assert extent % kt == 0
    qblk = lambda b, j: (b, qb0 + j, 0)
    head = lambda b, j: (b, 0, 0)
    return pl.pallas_call(
        functools.partial(_dsa_kernel, tq=tq, kt=kt, nkt=extent // kt, qb0=qb0, topk=topk, idx_bits=idx_bits),
        grid=(bsz, nqb),
        in_specs=[
            pl.BlockSpec((1, tq, DSA_W), qblk),
            pl.BlockSpec((1, tq, IDX_W), qblk),
            pl.BlockSpec((1, IDX_HEADS, tq), lambda b, j: (b, 0, qb0 + j)),
            pl.BlockSpec((1, extent, DSA_W), head),
            pl.BlockSpec((1, DSA_W, extent), head),
            pl.BlockSpec((1, extent, LANES), head),
        ],
        out_specs=pl.BlockSpec((1, tq, DSA_W), lambda b, j: (b, j, 0)),
        out_shape=jax.ShapeDtypeStruct((bsz, nqb * tq, DSA_W), BF16),
        scratch_shapes=[
            pltpu.VMEM((extent, tq), I32),
            pltpu.VMEM((extent, tq), F32),
            pltpu.VMEM((extent, DSA_HEADS * tq), F32),
            pltpu.VMEM((extent, DSA_HEADS * tq), BF16),
            pltpu.VMEM((DSA_W, tq), F32),
            pltpu.VMEM((1, tq), I32),
        ],
        compiler_params=_cparams(("parallel", "arbitrary")),
        name="dsa_prompt",
    )(dq, iq, iwt, dkb, vtb, ikb)


def _dsa_prompt(dq, iq, iwt, dkb, vtb, ikb, tq, kt):
    length = dq.shape[1]
    topk = min(IDX_TOPK, length // 4)
    nqb = kt // tq
    assert length % kt == 0 and kt % tq == 0
    idx_bits = int(length - 1).bit_length()
    outs = [_dsa_prompt_bucket(dq, iq, iwt, dkb, vtb, ikb, tq, kt, qb0, nqb, topk, idx_bits)
            for qb0 in range(0, length // tq, nqb)]
    return jnp.concatenate(outs, axis=1)


def _paged_spec(block, n_pages, n_pg, i):
    nd = len(block)
    return pl.BlockSpec(block, lambda b, g, pt: (pt[b * n_pages + g * n_pg + i],) + (0,) * (nd - 1))


def _idx_sample_kernel(pt_ref, q_ref, w_ref, iknew_ref, *rest, n_pg):
    pages = rest[:n_pg]
    score_ref, snew_ref = rest[n_pg:]
    q = q_ref[0]
    w = w_ref[0]
    n_t = score_ref.shape[1]

    def head_sum(s):
        r = jnp.maximum(s, 0.0) * w
        return jnp.sum(r.reshape(n_t, IDX_HEADS, r.shape[1]), axis=1)

    kb = jnp.concatenate([pg[0] for pg in pages], axis=1).astype(BF16)
    score_ref[0] = head_sum(_dot(q, kb))

    @pl.when(pl.program_id(1) == 0)
    def _():
        snew_ref[0] = head_sum(_dot_nt(q, iknew_ref[0]))


def _idx_sample(page_table_flat, q_rows, w_rows, ik_new, kidx_pages, n_t, n_pages, n_pg):
    bsz = q_rows.shape[0]
    ngrp = n_pages // n_pg
    per_b = lambda b, g, pt: (b, 0, 0)
    grid_spec = pltpu.PrefetchScalarGridSpec(
        num_scalar_prefetch=1,
        grid=(bsz, ngrp),
        in_specs=[
            pl.BlockSpec((1, n_t * IDX_HEADS, IDX_DIM), per_b),
            pl.BlockSpec((1, n_t * IDX_HEADS, 1), per_b),
            pl.BlockSpec((1, LANES, IDX_DIM), per_b),
        ] + [_paged_spec((1, IDX_DIM, PAGE_SIZE), n_pages, n_pg, i) for i in range(n_pg)],
        out_specs=[
            pl.BlockSpec((1, n_t, n_pg * PAGE_SIZE), lambda b, g, pt: (b, 0, g)),
            pl.BlockSpec((1, n_t, LANES), per_b),
        ],
    )
    return pl.pallas_call(
        functools.partial(_idx_sample_kernel, n_pg=n_pg),
        grid_spec=grid_spec,
        out_shape=[
            jax.ShapeDtypeStruct((bsz, n_t, n_pages * PAGE_SIZE), F32),
            jax.ShapeDtypeStruct((bsz, n_t, LANES), F32),
        ],
        compiler_params=_cparams(("parallel", "arbitrary")),
        name="idx_sample",
    )(page_table_flat, q_rows, w_rows, ik_new, *([kidx_pages] * n_pg))


def _thr_sample_kernel(score_ref, snew_ref, thr_ref, pcut_ref, key_scr, pcut_scr, *, n_t, past, topk):
    rows, width = key_scr.shape
    row0 = pl.program_id(0) * rows
    kpos = lax.broadcasted_iota(I32, (rows, width), 1)
    qpos = past + (row0 + lax.broadcasted_iota(I32, (rows, width), 0)) % n_t
    causal = kpos <= qpos
    key_scr[:, :past] = _sort_key(score_ref[...], causal[:, :past])
    key_scr[:, past:] = _sort_key(snew_ref[...], causal[:, past:])

    def count(pred):
        return jnp.sum(jnp.where(pred(key_scr[...], kpos), 1.0, 0.0), axis=1, keepdims=True)

    thr, pcut = _topk_threshold(count, (rows, 1), topk, pcut_scr, int(width - 1).bit_length())
    thr_ref[...] = thr
    pcut_ref[...] = pcut


def _thr_sample(score2d, snew2d, n_t, past, topk, rb):
    m = score2d.shape[0]
    width = past + LANES
    row = lambda i: (i, 0)
    return pl.pallas_call(
        functools.partial(_thr_sample_kernel, n_t=n_t, past=past, topk=topk),
        grid=(m // rb,),
        in_specs=[pl.BlockSpec((rb, past), row), pl.BlockSpec((rb, LANES), row)],
        out_specs=[pl.BlockSpec((rb, 1), row), pl.BlockSpec((rb, 1), row)],
        out_shape=[jax.ShapeDtypeStruct((m, 1), I32), jax.ShapeDtypeStruct((m, 1), I32)],
        scratch_shapes=[pltpu.VMEM((rb, width), I32), pltpu.VMEM((rb, 1), I32)],
        compiler_params=_cparams(("parallel",)),
        name="thr_sample",
    )(score2d, snew2d)


def _attn_sample_kernel(pt_ref, q_ref, score_ref, snew_ref, thr_ref, pcut_ref, knew_ref, vnew_ref, *rest,
                        n_pg, n_t, past, n_new):
    kpages = rest[:n_pg]
    vpages = rest[n_pg:2 * n_pg]
    o_ref, m_scr, l_scr, acc_scr = rest[2 * n_pg:]
    g = pl.program_id(1)
    rows = n_t * DSA_HEADS
    width = n_pg * PAGE_SIZE

    @pl.when(g == 0)
    def _():
        m_scr[...] = jnp.full(m_scr.shape, -jnp.inf, F32)
        l_scr[...] = jnp.zeros(l_scr.shape, F32)
        acc_scr[...] = jnp.zeros(acc_scr.shape, F32)

    thr = thr_ref[0]
    pcut = pcut_ref[0]
    qpos = past + lax.broadcasted_iota(I32, (n_t, 1), 0)

    def per_head(a):
        return jnp.broadcast_to(a[:, None, :], (n_t, DSA_HEADS, a.shape[1])).reshape(rows, a.shape[1])

    def bias_of(score, kpos):
        causal = kpos <= qpos
        sel = _select(_sort_key(score, causal), thr, pcut, kpos, causal)
        return per_head(jnp.where(sel, 0.0, -jnp.inf))

    head_of_row = lax.broadcasted_iota(I32, (rows, DSA_W), 0) % DSA_HEADS
    head_of_col = lax.broadcasted_iota(I32, (rows, DSA_W), 1) // DSA_HEAD_DIM
    diag = head_of_row == head_of_col
    qbd = jnp.where(diag, per_head(q_ref[0]), 0.0).astype(BF16)

    def update(logits, pv):
        m_old = m_scr[...]
        m_new = jnp.maximum(m_old, jnp.max(logits, axis=1, keepdims=True))
        m_safe = jnp.where(m_new == -jnp.inf, 0.0, m_new)
        e = jnp.exp(logits - m_safe)
        alpha = jnp.exp(m_old - m_safe)
        l_scr[...] = alpha * l_scr[...] + jnp.sum(e, axis=1, keepdims=True)
        acc_scr[...] = alpha * acc_scr[...] + pv(e.astype(BF16))
        m_scr[...] = m_new

    kt = jnp.concatenate([pg[0].reshape(DSA_W, PAGE_SIZE) for pg in kpages], axis=1).astype(BF16)
    vt = jnp.concatenate([pg[0].reshape(DSA_W, PAGE_SIZE) for pg in vpages], axis=1).astype(BF16)
    kpos = g * width + lax.broadcasted_iota(I32, (n_t, width), 1)
    update(_dot(qbd, kt) + bias_of(score_ref[0], kpos), lambda e: _dot_nt(e, vt))

    @pl.when(g == pl.num_programs(1) - 1)
    def _():
        kpos_new = past + lax.broadcasted_iota(I32, (n_t, n_new), 1)
        bias_new = bias_of(snew_ref[0][:, :n_new], kpos_new)
        update(_dot_nt(qbd, knew_ref[0]) + bias_new, lambda e: _dot(e, vnew_ref[0]))
        o = jnp.where(diag, acc_scr[...] / l_scr[...], 0.0)
        o_ref[0] = jnp.sum(o.reshape(n_t, DSA_HEADS, DSA_W), axis=1)


def _attn_sample(page_table_flat, q, score, snew, thr, pcut, knew, vnew, k_pages, v_pages,
                 n_t, n_pages, n_pg, past):
    bsz = q.shape[0]
    ngrp = n_pages // n_pg
    n_new = knew.shape[1]
    rows = n_t * DSA_HEADS
    per_b = lambda b, g, pt: (b, 0, 0)
    page_block = (1, DSA_HEADS, DSA_HEAD_DIM, PAGE_SIZE)
    grid_spec = pltpu.PrefetchScalarGridSpec(
        num_scalar_prefetch=1,
        grid=(bsz, ngrp),
        in_specs=[
            pl.BlockSpec((1, n_t, DSA_W), per_b),
            pl.BlockSpec((1, n_t, n_pg * PAGE_SIZE), lambda b, g, pt: (b, 0, g)),
            pl.BlockSpec((1, n_t, LANES), per_b),
            pl.BlockSpec((1, n_t, 1), per_b),
            pl.BlockSpec((1, n_t, 1), per_b),
            pl.BlockSpec((1, n_new, DSA_W), per_b),
            pl.BlockSpec((1, n_new, DSA_W), per_b),
        ] + [_paged_spec(page_block, n_pages, n_pg, i) for i in range(n_pg)] * 2,
        out_specs=pl.BlockSpec((1, n_t, DSA_W), per_b),
        scratch_shapes=[
            pltpu.VMEM((rows, 1), F32),
            pltpu.VMEM((rows, 1), F32),
            pltpu.VMEM((rows, DSA_W), F32),
        ],
    )
    return pl.pallas_call(
        functools.partial(_attn_sample_kernel, n_pg=n_pg, n_t=n_t, past=past, n_new=n_new),
        grid_spec=grid_spec,
        out_shape=jax.ShapeDtypeStruct((bsz, n_t, DSA_W), F32),
        compiler_params=_cparams(("parallel", "arbitrary")),
        name="attn_sample",
    )(page_table_flat, q, score, snew, thr, pcut, knew, vnew,
      *([k_pages] * n_pg), *([v_pages] * n_pg))


def _layer_norm(x, g, b):
    mu = jnp.mean(x, axis=-1, keepdims=True)
    xc = x - mu
    var = jnp.mean(xc * xc, axis=-1, keepdims=True)
    return xc * lax.rsqrt(var + NORM_EPS) * g + b


def _post_kernel(x_ref, ga_ref, ds_ref, wo_ref, l1g_ref, l1b_ref, wg_ref, wu_ref, wd_ref,
                 l2g_ref, l2b_ref, y_ref, *, alpha):
    mix = _dot(ga_ref[...], wo_ref[:GV_W, :]) + _dot(ds_ref[...], wo_ref[GV_W:, :])
    h = _layer_norm(alpha * x_ref[...] + mix, l1g_ref[...], l1b_ref[...])
    hb = h.astype(BF16)
    a = _dot(hb, wg_ref[...])
    u = _dot(hb, wu_ref[...])
    f = _dot((a * jax.nn.sigmoid(a) * u).astype(BF16), wd_ref[...])
    y_ref[...] = _layer_norm(alpha * h + f, l2g_ref[...], l2b_ref[...])


def _post(x2d, gated, dsa, wo, l1g, l1b, wg, wu, wd, l2g, l2b, alpha, tm):
    m, d = x2d.shape
    dff = wg.shape[1]
    tm = min(tm, m)
    assert m % tm == 0
    row = lambda i: (i, 0)
    return pl.pallas_call(
        functools.partial(_post_kernel, alpha=alpha),
        grid=(m // tm,),
        in_specs=[
            pl.BlockSpec((tm, d), row),
            pl.BlockSpec((tm, GV_W), row),
            pl.BlockSpec((tm, DSA_W), row),
            _const_spec((GV_W + DSA_W, d)),
            _const_spec((1, d)),
            _const_spec((1, d)),
            _const_spec((d, dff)),
            _const_spec((d, dff)),
            _const_spec((dff, d)),
            _const_spec((1, d)),
            _const_spec((1, d)),
        ],
        out_specs=pl.BlockSpec((tm, d), row),
        out_shape=jax.ShapeDtypeStruct((m, d), F32),
        compiler_params=_cparams(("parallel",)),
        name="post",
    )(x2d, gated, dsa, wo, l1g, l1b, wg, wu, wd, l2g, l2b)


def _layer_weights(l, w_in, w_lr_up, b_gate, gla_norm_g, w_o, ln1_g, ln1_b,
                   w_ffn_gate, w_ffn_up, w_ffn_down, ln2_g, ln2_b):
    row = lambda a: a[l][None, :].astype(F32)
    w, wvt = _prep_w_in(w_in[l])
    return dict(
        w=w, wvt=wvt, wlr=_prep_w_lr(w_lr_up[l]), bg=row(b_gate), gn=row(gla_norm_g),
        wo=w_o[l].astype(BF16), l1g=row(ln1_g), l1b=row(ln1_b),
        wg=w_ffn_gate[l].astype(BF16), wu=w_ffn_up[l].astype(BF16), wd=w_ffn_down[l].astype(BF16),
        l2g=row(ln2_g), l2b=row(ln2_b))


def _prompt_layer(x, wts, alpha):
    bsz, length, d = x.shape
    tm = 512 if length % 512 == 0 else length
    tabs = _rope_tables(jnp.arange(length, dtype=F32))
    x2d = x.reshape(bsz * length, d)
    (gq, gk, gv, gr, gg, dq, dkb, iq, ikb, kt, vt, vtb, ikt, iwt) = _project_fm(
        x2d, wts["w"], wts["wvt"], wts["wlr"], wts["bg"], *tabs, tm, bsz)
    r3 = lambda a: a.reshape(bsz, length, a.shape[-1])
    chunk = int(np.gcd(length, GLA_CHUNK))
    n_sub = 4 if length % (4 * chunk) == 0 else 1
    s0 = jnp.zeros((bsz, GLA_HEADS, GLA_DK, GLA_DV), F32)
    gated, state = _gla(r3(gq), r3(gk), r3(gg), r3(gv), r3(gr), s0, wts["gn"], chunk, n_sub, 1)
    kt_tile = 256 if length % 256 == 0 else 128
    dsa = _dsa_prompt(r3(dq), r3(iq), iwt, r3(dkb), vtb, r3(ikb), 128, kt_tile)
    y = _post(x2d, gated.reshape(bsz * length, GV_W), dsa.reshape(bsz * length, DSA_W),
              wts["wo"], wts["l1g"], wts["l1b"], wts["wg"], wts["wu"], wts["wd"],
              wts["l2g"], wts["l2b"], alpha, 256)
    heads = lambda a: jnp.transpose(a.reshape(bsz, DSA_HEADS, DSA_HEAD_DIM, length), (0, 3, 1, 2))
    return (y.reshape(bsz, length, d), heads(kt), heads(vt), jnp.transpose(ikt, (0, 2, 1)), state)


def _sample_layer(x, wts, alpha, cache_k, cache_v, cache_kidx, state, page_table):
    bsz, n_t, d = x.shape
    n_pages = page_table.shape[1]
    past = n_pages * PAGE_SIZE
    m = bsz * n_t
    pos = past + jnp.arange(n_t, dtype=F32)
    tabs = [jnp.tile(t, (bsz, 1)) for t in _rope_tables(pos)]
    x2d = x.reshape(m, d)
    (gq, gk, gv, gr, gg, dq, dkb, iq, ikb, dk, dv, dvb, ik, iw) = _project(
        x2d, wts["w"], wts["wlr"], wts["bg"], *tabs, m)
    chunk = 16
    n_seq = 8 if bsz % 8 == 0 else 1
    pad3 = lambda a: jnp.pad(a.reshape(bsz, n_t, a.shape[-1]), ((0, 0), (0, chunk - n_t), (0, 0)))
    gated, state_new = _gla(pad3(gq), pad3(gk), pad3(gg), pad3(gv), pad3(gr), state, wts["gn"],
                            chunk, 1, n_seq)
    gated = gated[:, :n_t].reshape(m, GV_W)
    topk = min(IDX_TOPK, (past + n_t) // 4)
    pick = lambda want: max(c for c in (1, 2, 4, 8, 16, 32) if c <= want and n_pages % c == 0)
    pt_flat = page_table.reshape(-1)
    q_rows = iq.reshape(bsz, n_t * IDX_HEADS, IDX_DIM)
    w_rows = iw.reshape(bsz, n_t * IDX_HEADS, 1)
    ik_new = jnp.pad(ikb[:, :IDX_DIM].reshape(bsz, n_t, IDX_DIM), ((0, 0), (0, LANES - n_t), (0, 0)))
    score, snew = _idx_sample(pt_flat, q_rows, w_rows, ik_new, cache_kidx, n_t, n_pages, pick(32))
    thr, pcut = _thr_sample(score.reshape(m, past), snew.reshape(m, LANES), n_t, past, topk, 32)
    n_new = 16
    new3 = lambda a: jnp.pad(a.reshape(bsz, n_t, DSA_W), ((0, 0), (0, n_new - n_t), (0, 0)))
    dsa = _attn_sample(pt_flat, dq.astype(F32).reshape(bsz, n_t, DSA_W), score, snew,
                       thr.reshape(bsz, n_t, 1), pcut.reshape(bsz, n_t, 1), new3(dkb), new3(dvb),
                       cache_k, cache_v, n_t, n_pages, pick(16), past)
    y = _post(x2d, gated, dsa.reshape(m, DSA_W).astype(BF16),
              wts["wo"], wts["l1g"], wts["l1b"], wts["wg"], wts["wu"], wts["wd"],
              wts["l2g"], wts["l2b"], alpha, 256)
    return (y.reshape(bsz, n_t, d),
            dk.reshape(bsz, n_t, DSA_HEADS, DSA_HEAD_DIM),
            dv.reshape(bsz, n_t, DSA_HEADS, DSA_HEAD_DIM),
            ik.reshape(bsz, n_t, IDX_DIM), state_new)


def kernel(x_prompt, x_sample, cache_k, cache_v, cache_kidx, state_gla, page_table, w_in, w_lr_up, b_gate, gla_norm_g, w_o, ln1_g, ln1_b, w_ffn_gate, w_ffn_up, w_ffn_down, ln2_g, ln2_b):
    depth = w_in.shape[0]
    alpha = float((2 * depth) ** 0.25)
    n_pool = cache_k.shape[1]
    xp, xs = x_prompt, x_sample
    outs_p = [[] for _ in range(4)]
    outs_s = [[] for _ in range(4)]
    for l in range(depth):
        wts = _layer_weights(l, w_in, w_lr_up, b_gate, gla_norm_g, w_o, ln1_g, ln1_b,
                             w_ffn_gate, w_ffn_up, w_ffn_down, ln2_g, ln2_b)
        xp, *rest_p = _prompt_layer(xp, wts, alpha)
        feat = lambda c: jnp.transpose(c, (0, 1, 3, 4, 2)).reshape(
            depth * n_pool, DSA_HEADS, DSA_HEAD_DIM, PAGE_SIZE)
        xs, *rest_s = _sample_layer(
            xs, wts, alpha, feat(cache_k), feat(cache_v),
            jnp.transpose(cache_kidx, (0, 1, 3, 2)).reshape(depth * n_pool, IDX_DIM, PAGE_SIZE),
            state_gla[l], page_table + l * n_pool)
        for acc, val in zip(outs_p, rest_p):
            acc.append(val)
        for acc, val in zip(outs_s, rest_s):
            acc.append(val)
    stack = lambda seq: jnp.stack(seq)
    return (xp, xs, *[stack(a) for a in outs_p], *[stack(a) for a in outs_s])
```

```python
import functools

import numpy as np
import jax
import jax.numpy as jnp
from jax import lax
from jax.experimental import pallas as pl
from jax.experimental.pallas import tpu as pltpu

F32 = jnp.float32
BF16 = jnp.bfloat16
I32 = jnp.int32

GLA_HEADS = 4
GLA_DK = 64
GLA_DV = 128
GLA_LR_RANK = 16
GATE_TAU = 16.0
GLA_CHUNK = 64
DSA_HEADS = 8
DSA_HEAD_DIM = 64
ROT_DIM = 16
IDX_HEADS = 8
IDX_DIM = 64
IDX_TOPK = 256
PAGE_SIZE = 128
ROPE_THETA = 500000.0
NORM_EPS = 1e-5

LANES = 128
SUBLANES = 8
ROW_CHUNK = 64
INT_MIN = int(np.iinfo(np.int32).min)
INT_MAX = int(np.iinfo(np.int32).max)
VMEM_LIMIT = 56 * 1024 * 1024

GQ_W = GLA_HEADS * GLA_DK
GV_W = GLA_HEADS * GLA_DV
DSA_W = DSA_HEADS * DSA_HEAD_DIM
IDX_W = IDX_HEADS * IDX_DIM
OFF_GQ = 0
OFF_GK = OFF_GQ + GQ_W
OFF_GV = OFF_GK + GQ_W
OFF_GR = OFF_GV + GV_W
OFF_DQ = OFF_GR + GV_W
OFF_DK = OFF_DQ + DSA_W
OFF_DV = OFF_DK + DSA_W
OFF_IQ = OFF_DV + DSA_W
OFF_MISC = OFF_IQ + IDX_W
N_COLS = OFF_MISC + LANES
MISC_GLR = IDX_DIM
MISC_IW = IDX_DIM + GLA_LR_RANK

NT_DIMS = (((1,), (1,)), ((), ()))
TN_DIMS = (((0,), (0,)), ((), ()))


def _dot(a, b):
    return jnp.dot(a, b, preferred_element_type=F32)


def _dot_nt(a, b):
    return lax.dot_general(a, b, NT_DIMS, preferred_element_type=F32)


def _dot_tn(a, b):
    return lax.dot_general(a, b, TN_DIMS, preferred_element_type=F32)


def _cparams(sem):
    return pltpu.CompilerParams(dimension_semantics=sem, vmem_limit_bytes=VMEM_LIMIT)


def _const_spec(shape):
    nd = len(shape)
    return pl.BlockSpec(shape, lambda *_: (0,) * nd, pipeline_mode=pl.Buffered(1))


def _proj_shared(x_ref, w_ref, wlr_ref, bg_ref, cos_ref, sina_ref, sinb_ref,
                 gq_ref, gk_ref, gv_ref, gr_ref, gg_ref, dq_ref, dkb_ref, iq_ref, ikb_ref):
    xb = x_ref[...].astype(BF16)
    cosp = cos_ref[...]
    sina = sina_ref[...]
    sinb = sinb_ref[...]

    def seg(off, width):
        return _dot(xb, w_ref[:, off:off + width])

    def rope(u):
        return u * cosp + pltpu.roll(u, LANES - ROT_DIM // 2, 1) * sina + pltpu.roll(u, ROT_DIM // 2, 1) * sinb

    gq_ref[...] = seg(OFF_GQ, GQ_W) * (GLA_DK ** -0.5)
    gk_ref[...] = seg(OFF_GK, GQ_W)
    gv_ref[...] = seg(OFF_GV, GV_W).astype(BF16)
    gr_ref[...] = seg(OFF_GR, GV_W).astype(BF16)
    uq = seg(OFF_DQ, DSA_W)
    uk = seg(OFF_DK, DSA_W)
    ui = seg(OFF_IQ, IDX_W)
    keys = []
    for i in range(DSA_W // LANES):
        sl = slice(i * LANES, (i + 1) * LANES)
        dq_ref[:, sl] = (rope(uq[:, sl]) * (DSA_HEAD_DIM ** -0.5)).astype(BF16)
        k = rope(uk[:, sl])
        dkb_ref[:, sl] = k.astype(BF16)
        keys.append(k)
        iq_ref[:, sl] = (rope(ui[:, sl]) * (IDX_DIM ** -0.5)).astype(BF16)
    misc = seg(OFF_MISC, LANES)
    mr = rope(misc)
    lane = lax.broadcasted_iota(I32, (1, LANES), 1)
    ikb_ref[...] = jnp.where(lane < IDX_DIM, mr, pltpu.roll(mr, IDX_DIM, 1)).astype(BF16)
    z = _dot(misc.astype(BF16), wlr_ref[...]) + bg_ref[...]
    gg_ref[...] = (jnp.minimum(z, 0.0) - jnp.log1p(jnp.exp(-jnp.abs(z)))) * (1.0 / GATE_TAU)
    return xb, keys, misc, mr


def _proj_kernel(x_ref, w_ref, wlr_ref, bg_ref, cos_ref, sina_ref, sinb_ref,
                 gq_ref, gk_ref, gv_ref, gr_ref, gg_ref, dq_ref, dkb_ref, iq_ref, ikb_ref,
                 dk_ref, dv_ref, dvb_ref, ik_ref, iw_ref):
    xb, keys, misc, mr = _proj_shared(x_ref, w_ref, wlr_ref, bg_ref, cos_ref, sina_ref, sinb_ref,
                                      gq_ref, gk_ref, gv_ref, gr_ref, gg_ref, dq_ref, dkb_ref, iq_ref, ikb_ref)
    for i, k in enumerate(keys):
        dk_ref[:, i * LANES:(i + 1) * LANES] = k
    v = _dot(xb, w_ref[:, OFF_DV:OFF_DV + DSA_W])
    dv_ref[...] = v
    dvb_ref[...] = v.astype(BF16)
    ik_ref[...] = mr[:, :IDX_DIM]
    iw_ref[...] = misc[:, MISC_IW:MISC_IW + IDX_HEADS] * (IDX_HEADS ** -0.5)


def _proj_fm_kernel(x_ref, w_ref, wvt_ref, wlr_ref, bg_ref, cos_ref, sina_ref, sinb_ref,
                    gq_ref, gk_ref, gv_ref, gr_ref, gg_ref, dq_ref, dkb_ref, iq_ref, ikb_ref,
                    kt_ref, vt_ref, vtb_ref, ikt_ref, iwt_ref):
    xb, keys, misc, mr = _proj_shared(x_ref, w_ref, wlr_ref, bg_ref, cos_ref, sina_ref, sinb_ref,
                                      gq_ref, gk_ref, gv_ref, gr_ref, gg_ref, dq_ref, dkb_ref, iq_ref, ikb_ref)
    for i, k in enumerate(keys):
        kt_ref[0, i * LANES:(i + 1) * LANES, :] = k.T
    vt = _dot_nt(wvt_ref[...], xb)
    vt_ref[0] = vt
    vtb_ref[0] = vt.astype(BF16)
    mt = mr.T
    ikt_ref[0] = mt[:IDX_DIM]
    iwt_ref[0] = mt[MISC_IW:MISC_IW + IDX_HEADS] * (IDX_HEADS ** -0.5)


_PROJ_SHARED_OUTS = [
    (GQ_W, F32), (GQ_W, F32), (GV_W, BF16), (GV_W, BF16), (GQ_W, F32),
    (DSA_W, BF16), (DSA_W, BF16), (IDX_W, BF16), (LANES, BF16),
]


def _project(x2d, w, wlr, bg, cos_t, sina_t, sinb_t, tm):
    m, d = x2d.shape
    period = cos_t.shape[0]
    assert m % tm == 0 and period % tm == 0
    nper = period // tm
    row = lambda i: (i, 0)
    tab = lambda i: (i % nper, 0)
    outs = _PROJ_SHARED_OUTS + [(DSA_W, F32), (DSA_W, F32), (DSA_W, BF16), (IDX_DIM, F32), (IDX_HEADS, F32)]
    return pl.pallas_call(
        _proj_kernel,
        grid=(m // tm,),
        in_specs=[
            pl.BlockSpec((tm, d), row),
            _const_spec((d, N_COLS)),
            _const_spec((LANES, GQ_W)),
            _const_spec((1, GQ_W)),
            pl.BlockSpec((tm, LANES), tab),
            pl.BlockSpec((tm, LANES), tab),
            pl.BlockSpec((tm, LANES), tab),
        ],
        out_specs=[pl.BlockSpec((tm, wd), row) for wd, _ in outs],
        out_shape=[jax.ShapeDtypeStruct((m, wd), dt) for wd, dt in outs],
        compiler_params=_cparams(("parallel",)),
        name="project",
    )(x2d, w, wlr, bg, cos_t, sina_t, sinb_t)


def _project_fm(x2d, w, wvt, wlr, bg, cos_t, sina_t, sinb_t, tm, bsz):
    m, d = x2d.shape
    length = m // bsz
    assert length % tm == 0 and cos_t.shape[0] == length
    nper = length // tm
    row = lambda i: (i, 0)
    tab = lambda i: (i % nper, 0)
    fm = lambda i: (i // nper, 0, i % nper)
    fm_outs = [(DSA_W, F32), (DSA_W, F32), (DSA_W, BF16), (IDX_DIM, F32), (IDX_HEADS, F32)]
    return pl.pallas_call(
        _proj_fm_kernel,
        grid=(m // tm,),
        in_specs=[
            pl.BlockSpec((tm, d), row),
            _const_spec((d, N_COLS)),
            _const_spec((DSA_W, d)),
            _const_spec((LANES, GQ_W)),
            _const_spec((1, GQ_W)),
            pl.BlockSpec((tm, LANES), tab),
            pl.BlockSpec((tm, LANES), tab),
            pl.BlockSpec((tm, LANES), tab),
        ],
        out_specs=[pl.BlockSpec((tm, wd), row) for wd, _ in _PROJ_SHARED_OUTS]
        + [pl.BlockSpec((1, wd, tm), fm) for wd, _ in fm_outs],
        out_shape=[jax.ShapeDtypeStruct((m, wd), dt) for wd, dt in _PROJ_SHARED_OUTS]
        + [jax.ShapeDtypeStruct((bsz, wd, length), dt) for wd, dt in fm_outs],
        compiler_params=_cparams(("parallel",)),
        name="project_fm",
    )(x2d, w, wvt, wlr, bg, cos_t, sina_t, sinb_t)


_IN_SIZES = (GQ_W, GQ_W, GV_W, GV_W, GLA_LR_RANK, DSA_W, DSA_W, DSA_W, IDX_W, IDX_DIM, IDX_HEADS)


def _prep_w_in(w_in):
    cuts = np.cumsum(_IN_SIZES)[:-1].tolist()
    gq, gk, gv, gr, glr, dq, dk, dv, iq, ik, iw = jnp.split(w_in, cuts, axis=1)
    pad = jnp.zeros((w_in.shape[0], LANES - IDX_DIM - GLA_LR_RANK - IDX_HEADS), w_in.dtype)
    w = jnp.concatenate([gq, gk, gv, gr, dq, dk, dv, iq, ik, glr, iw, pad], axis=1).astype(BF16)
    return w, jnp.transpose(dv).astype(BF16)


def _prep_w_lr(w_lr_up):
    z0 = jnp.zeros((MISC_GLR, GQ_W), w_lr_up.dtype)
    z1 = jnp.zeros((LANES - MISC_GLR - GLA_LR_RANK, GQ_W), w_lr_up.dtype)
    return jnp.concatenate([z0, w_lr_up, z1], axis=0).astype(BF16)


def _rope_tables(pos):
    half = ROT_DIM // 2
    inv = jnp.power(ROPE_THETA, -jnp.arange(half, dtype=F32) * (2.0 / ROT_DIM))
    ang = pos[:, None] * inv[None, :]
    cos = jnp.cos(ang)
    sin = jnp.sin(ang)
    n = pos.shape[0]
    rest = DSA_HEAD_DIM - ROT_DIM
    z8 = jnp.zeros((n, half), F32)
    zr = jnp.zeros((n, rest), F32)
    cos_h = jnp.concatenate([cos, cos, jnp.ones((n, rest), F32)], axis=1)
    sina_h = jnp.concatenate([-sin, z8, zr], axis=1)
    sinb_h = jnp.concatenate([z8, sin, zr], axis=1)
    rep = LANES // DSA_HEAD_DIM
    return jnp.tile(cos_h, (1, rep)), jnp.tile(sina_h, (1, rep)), jnp.tile(sinb_h, (1, rep))


def _gla_kernel(gq_ref, gk_ref, gg_ref, gv_ref, gr_ref, s0_ref, gn_ref, o_ref, sout_ref, st_scr,
                *, chunk, n_sub, n_seq):
    j = pl.program_id(1)
    lane = lax.broadcasted_iota(I32, (1, LANES), 1)
    lo = lane < GLA_DK

    @pl.when(j == 0)
    def _():
        zero = jnp.zeros((GLA_DK, GLA_DV), F32)
        for s in range(n_seq):
            for h in range(GLA_HEADS):
                s0 = s0_ref[s, h]
                full = jnp.concatenate([s0, zero] if h % 2 == 0 else [zero, s0], axis=0)
                st_scr[s * GLA_HEADS + h] = full.T

    r = lax.broadcasted_iota(I32, (chunk, chunk), 0)
    c = lax.broadcasted_iota(I32, (chunk, chunk), 1)
    tri = r >= c
    trib = jnp.where(tri, 1.0, 0.0).astype(BF16)
    gn = gn_ref[...]
    for s in range(n_seq):
        for ci in range(n_sub):
            rows = slice(ci * chunk, (ci + 1) * chunk)
            g = gg_ref[s, rows, :]
            g_hi = g.astype(BF16)
            g_lo = (g - g_hi.astype(F32)).astype(BF16)
            b = _dot(trib, g_hi) + _dot(trib, g_lo)
            b_last = b[chunk - 1:chunk, :]
            q = gq_ref[s, rows, :] * jnp.exp(b)
            k = gk_ref[s, rows, :]
            k_in = k * jnp.exp(-b)
            k_out = k * jnp.exp(b_last - b)
            d_last = jnp.exp(b_last)
            for h in range(GLA_HEADS):
                p, hh = divmod(h, 2)
                sl = slice(p * LANES, (p + 1) * LANES)
                vsl = slice(h * GLA_DV, (h + 1) * GLA_DV)
                msk = lo if hh == 0 else jnp.logical_not(lo)
                qm = jnp.where(msk, q[:, sl], 0.0).astype(BF16)
                a = _dot_nt(qm, k_in[:, sl].astype(BF16))
                a = jnp.where(tri, a, 0.0).astype(BF16)
                vh = gv_ref[s, rows, vsl]
                st = st_scr[s * GLA_HEADS + h]
                o = _dot(a, vh) + _dot_nt(qm, st.astype(BF16))
                km = jnp.where(msk, k_out[:, sl], 0.0).astype(BF16)
                st_scr[s * GLA_HEADS + h] = st * d_last[:, sl] + _dot_tn(vh, km)
                ms = jnp.mean(o * o, axis=1, keepdims=True)
                of = o * lax.rsqrt(ms + NORM_EPS) * gn
                gr = gr_ref[s, rows, vsl].astype(F32)
                o_ref[s, rows, vsl] = (of * (gr * jax.nn.sigmoid(gr))).astype(BF16)

    @pl.when(j == pl.num_programs(1) - 1)
    def _():
        for s in range(n_seq):
            for h in range(GLA_HEADS):
                off = (h % 2) * GLA_DK
                sout_ref[s, h] = st_scr[s * GLA_HEADS + h].T[off:off + GLA_DK, :]


def _gla(gq, gk, gg, gv, gr, s0, gn, chunk, n_sub, n_seq):
    bsz, length, _ = gq.shape
    tl = chunk * n_sub
    assert length % tl == 0 and bsz % n_seq == 0
    tok = lambda b, j: (b, j, 0)
    st = lambda b, j: (b, 0, 0, 0)
    return pl.pallas_call(
        functools.partial(_gla_kernel, chunk=chunk, n_sub=n_sub, n_seq=n_seq),
        grid=(bsz // n_seq, length // tl),
        in_specs=[
            pl.BlockSpec((n_seq, tl, GQ_W), tok),
            pl.BlockSpec((n_seq, tl, GQ_W), tok),
            pl.BlockSpec((n_seq, tl, GQ_W), tok),
            pl.BlockSpec((n_seq, tl, GV_W), tok),
            pl.BlockSpec((n_seq, tl, GV_W), tok),
            pl.BlockSpec((n_seq, GLA_HEADS, GLA_DK, GLA_DV), st),
            pl.BlockSpec((1, GLA_DV), lambda b, j: (0, 0)),
        ],
        out_specs=[
            pl.BlockSpec((n_seq, tl, GV_W), tok),
            pl.BlockSpec((n_seq, GLA_HEADS, GLA_DK, GLA_DV), st),
        ],
        out_shape=[
            jax.ShapeDtypeStruct((bsz, length, GV_W), BF16),
            jax.ShapeDtypeStruct((bsz, GLA_HEADS, GLA_DK, GLA_DV), F32),
        ],
        scratch_shapes=[pltpu.VMEM((n_seq * GLA_HEADS, GLA_DV, LANES), F32)],
        compiler_params=_cparams(("parallel", "arbitrary")),
        name="gla",
    )(gq, gk, gg, gv, gr, s0, gn)


def _sort_key(score, causal):
    sc = jnp.where(score == 0.0, 0.0, score)
    bits = pltpu.bitcast(sc, I32)
    key = bits ^ ((bits >> 31) & INT_MAX)
    return jnp.where(causal, key, INT_MIN)


def _topk_threshold(count, shape, topk, pcut_ref, idx_bits):
    kf = float(topk)
    cnt0 = count(lambda key, kpos: key >= 0)
    thr0 = jnp.where(cnt0 >= kf, 0, INT_MIN).astype(I32)

    def body(i, thr):
        cand = thr + jnp.left_shift(jnp.int32(1), 30 - i)
        cnt = count(lambda key, kpos: key >= cand)
        return jnp.where(cnt >= kf, cand, thr)

    thr = lax.fori_loop(0, 31, body, thr0)
    need = kf - count(lambda key, kpos: key > thr)
    n_eq = count(lambda key, kpos: key == thr)
    cut = jnp.logical_and(thr > INT_MIN, n_eq > need)
    pcut_ref[...] = jnp.full(shape, INT_MAX, I32)

    @pl.when(jnp.max(jnp.where(cut, 1.0, 0.0)) > 0.0)
    def _():
        def pbody(i, pos):
            cand = pos + jnp.left_shift(jnp.int32(1), idx_bits - 1 - i)
            cnt = count(lambda key, kpos: jnp.logical_and(key == thr, kpos < cand))
            return jnp.where(cnt < need, cand, pos)

        pos = lax.fori_loop(0, idx_bits, pbody, jnp.zeros(shape, I32))
        pcut_ref[...] = jnp.where(cut, pos, INT_MAX)

    return thr, pcut_ref[...]


def _select(key, thr, pcut, kpos, causal):
    keep = jnp.logical_or(key > thr, jnp.logical_and(key == thr, kpos <= pcut))
    return jnp.logical_and(keep, causal)


def _dsa_kernel(dq_ref, iq_ref, iwt_ref, k_ref, vt_ref, ik_ref, o_ref,
                key_scr, bias_scr, logit_scr, p_scr, out_scr, pcut_scr, *, tq, kt, nkt, qb0, topk, idx_bits):
    j = pl.program_id(1)
    t0 = (qb0 + j) * tq
    lane = lax.broadcasted_iota(I32, (1, LANES), 1)
    lo = lane < DSA_HEAD_DIM
    qpos = t0 + lax.broadcasted_iota(I32, (1, tq), 1)
    wt = iwt_ref[0]

    def head_pair(ref, p):
        qs = ref[0, :, p * LANES:(p + 1) * LANES]
        zero = jnp.zeros_like(qs)
        return jnp.concatenate([jnp.where(lo, qs, zero), jnp.where(lo, zero, qs)], axis=0)

    def rows_of(c):
        return pl.ds(pl.multiple_of(c * kt, kt), kt)

    def kpos_of(c):
        return c * kt + lax.broadcasted_iota(I32, (kt, tq), 0)

    def col_sum(x):
        return jnp.sum(x.reshape(kt // SUBLANES, SUBLANES, tq), axis=0)

    n_pair = IDX_HEADS // 2
    iq2 = [head_pair(iq_ref, p) for p in range(n_pair)]
    w2 = [jnp.concatenate([wt[2 * p:2 * p + 1, :], wt[2 * p + 1:2 * p + 2, :]], axis=1) for p in range(n_pair)]

    def score_tile(c, carry):
        rs = pl.ds(pl.multiple_of(c * tq, tq), tq)
        ik_t = ik_ref[0, rs, :]
        acc2 = jnp.zeros((tq, 2 * tq), F32)
        for p in range(n_pair):
            acc2 = acc2 + jnp.maximum(_dot_nt(ik_t, iq2[p]), 0.0) * w2[p]
        kpos = c * tq + lax.broadcasted_iota(I32, (tq, tq), 0)
        key_scr[rs, :] = _sort_key(acc2[:, :tq] + acc2[:, tq:], kpos <= qpos)
        return carry

    lax.fori_loop(0, nkt * kt // tq, score_tile, 0, unroll=4 if (nkt * kt // tq) % 4 == 0 else 2)

    def count(pred):
        acc = jnp.zeros((SUBLANES, tq), F32)
        for c in range(nkt):
            hit = pred(key_scr[c * kt:(c + 1) * kt, :], kpos_of(c))
            acc = acc + col_sum(jnp.where(hit, 1.0, 0.0))
        return jnp.sum(acc, axis=0, keepdims=True)

    thr, pcut = _topk_threshold(count, (1, tq), topk, pcut_scr, idx_bits)

    def bias_tile(c, carry):
        kpos = kpos_of(c)
        causal = kpos <= qpos
        sel = _select(key_scr[rows_of(c), :], thr, pcut, kpos, causal)
        bias_scr[rows_of(c), :] = jnp.where(sel, 0.0, -jnp.inf)
        return carry

    lax.fori_loop(0, nkt, bias_tile, 0)

    extent = nkt * kt
    n_chunk = extent // ROW_CHUNK
    n_pair = DSA_HEADS // 2
    q2s = [head_pair(dq_ref, p) for p in range(n_pair)]

    def chunk_rows(i):
        return pl.ds(pl.multiple_of(i * ROW_CHUNK, ROW_CHUNK), ROW_CHUNK)

    def col_max(x):
        return jnp.max(x.reshape(kt // SUBLANES, SUBLANES, tq), axis=0)

    def logit_body(c, carry):
        rs = rows_of(c)
        b = bias_scr[rs, :]
        out = []
        for p in range(n_pair):
            x = _dot_nt(k_ref[0, rs, p * LANES:(p + 1) * LANES], q2s[p])
            xa = x[:, :tq] + b
            xb = x[:, tq:] + b
            logit_scr[rs, 2 * p * tq:(2 * p + 2) * tq] = jnp.concatenate([xa, xb], axis=1)
            out.append(jnp.maximum(carry[2 * p], col_max(xa)))
            out.append(jnp.maximum(carry[2 * p + 1], col_max(xb)))
        return tuple(out)

    neg = jnp.full((SUBLANES, tq), -jnp.inf, F32)
    m8 = lax.fori_loop(0, nkt, logit_body, (neg,) * DSA_HEADS, unroll=2 if nkt % 2 == 0 else 1)
    m_all = jnp.concatenate([jnp.max(m, axis=0, keepdims=True) for m in m8], axis=1)

    def exp_body(i, l8):
        e = jnp.exp(logit_scr[chunk_rows(i), :] - m_all)
        p_scr[chunk_rows(i), :] = e.astype(BF16)
        return l8 + jnp.sum(e.reshape(ROW_CHUNK // SUBLANES, SUBLANES, DSA_HEADS * tq), axis=0)

    l8 = lax.fori_loop(0, n_chunk, exp_body, jnp.zeros((SUBLANES, DSA_HEADS * tq), F32), unroll=2)
    l_all = jnp.sum(l8, axis=0, keepdims=True)
    for h in range(DSA_HEADS):
        vsl = slice(h * DSA_HEAD_DIM, (h + 1) * DSA_HEAD_DIM)
        qsl = slice(h * tq, (h + 1) * tq)
        out_scr[vsl, :] = _dot(vt_ref[0, vsl, :], p_scr[:, qsl]) / l_all[:, qsl]

    o_ref[0] = out_scr[...].T.astype(BF16)


def _dsa_prompt_bucket(dq, iq, iwt, dkb, vtb, ikb, tq, kt, qb0, nqb, topk, idx_bits):
    bsz = dq.shape[0]
    extent = (qb0 + nqb) * tq
    assert extent % kt == 0
    qblk = lambda b, j: (b, qb0 + j, 0)
    head = lambda b, j: (b, 0, 0)
    return pl.pallas_call(
        functools.partial(_dsa_kernel, tq=tq, kt=kt, nkt=extent // kt, qb0=qb0, topk=topk, idx_bits=idx_bits),
        grid=(bsz, nqb),
        in_specs=[
            pl.BlockSpec((1, tq, DSA_W), qblk),
            pl.BlockSpec((1, tq, IDX_W), qblk),
            pl.BlockSpec((1, IDX_HEADS, tq), lambda b, j: (b, 0, qb0 + j)),
            pl.BlockSpec((1, extent, DSA_W), head),
            pl.BlockSpec((1, DSA_W, extent), head),
            pl.BlockSpec((1, extent, LANES), head),
        ],
        out_specs=pl.BlockSpec((1, tq, DSA_W), lambda b, j: (b, j, 0)),
        out_shape=jax.ShapeDtypeStruct((bsz, nqb * tq, DSA_W), BF16),
        scratch_shapes=[
            pltpu.VMEM((extent, tq), I32),
            pltpu.VMEM((extent, tq), F32),
            pltpu.VMEM((extent, DSA_HEADS * tq), F32),
            pltpu.VMEM((extent, DSA_HEADS * tq), BF16),
            pltpu.VMEM((DSA_W, tq), F32),
            pltpu.VMEM((1, tq), I32),
        ],
        compiler_params=_cparams(("parallel", "arbitrary")),
        name="dsa_prompt",
    )(dq, iq, iwt, dkb, vtb, ikb)


def _dsa_prompt(dq, iq, iwt, dkb, vtb, ikb, tq, kt):
    length = dq.shape[1]
    topk = min(IDX_TOPK, length // 4)
    nqb = kt // tq
    assert length % kt == 0 and kt % tq == 0
    idx_bits = int(length - 1).bit_length()
    outs = [_dsa_prompt_bucket(dq, iq, iwt, dkb, vtb, ikb, tq, kt, qb0, nqb, topk, idx_bits)
            for qb0 in range(0, length // tq, nqb)]
    return jnp.concatenate(outs, axis=1)


def _fetch_pages(pt_ref, hbm_refs, bufs, sems, n_pg):
    ngrp = pl.num_programs(1)
    s = pl.program_id(0) * ngrp + pl.program_id(1)
    total = pl.num_programs(0) * ngrp
    slot = s % 2

    def copies(step, slot_):
        return [pltpu.make_async_copy(hbm.at[pt_ref[step * n_pg + i]], buf.at[slot_, i], sems.at[slot_, a])
                for a, (hbm, buf) in enumerate(zip(hbm_refs, bufs)) for i in range(n_pg)]

    @pl.when(s == 0)
    def _():
        for c in copies(0, 0):
            c.start()

    @pl.when(s + 1 < total)
    def _():
        for c in copies(s + 1, 1 - slot):
            c.start()

    for c in copies(s, slot):
        c.wait()
    return slot


def _idx_sample_kernel(pt_ref, q_ref, w_ref, iknew_ref, kidx_hbm, score_ref, snew_ref, page_buf, sems, *, n_pg):
    slot = _fetch_pages(pt_ref, [kidx_hbm], [page_buf], sems, n_pg)
    q = q_ref[0]
    w = w_ref[0]
    n_t = score_ref.shape[1]

    def head_sum(s):
        r = jnp.maximum(s, 0.0) * w
        return jnp.sum(r.reshape(n_t, IDX_HEADS, r.shape[1]), axis=1)

    for i in range(n_pg):
        score_ref[0, :, i * PAGE_SIZE:(i + 1) * PAGE_SIZE] = head_sum(_dot(q, page_buf[slot, i].astype(BF16)))

    @pl.when(pl.program_id(1) == 0)
    def _():
        snew_ref[0] = head_sum(_dot_nt(q, iknew_ref[0]))


def _idx_sample(page_table_flat, q_rows, w_rows, ik_new, kidx_pages, n_t, n_pages, n_pg):
    bsz = q_rows.shape[0]
    ngrp = n_pages // n_pg
    per_b = lambda b, g, pt: (b, 0, 0)
    grid_spec = pltpu.PrefetchScalarGridSpec(
        num_scalar_prefetch=1,
        grid=(bsz, ngrp),
        in_specs=[
            pl.BlockSpec((1, n_t * IDX_HEADS, IDX_DIM), per_b),
            pl.BlockSpec((1, n_t * IDX_HEADS, 1), per_b),
            pl.BlockSpec((1, LANES, IDX_DIM), per_b),
            pl.BlockSpec(memory_space=pl.ANY),
        ],
        out_specs=[
            pl.BlockSpec((1, n_t, n_pg * PAGE_SIZE), lambda b, g, pt: (b, 0, g)),
            pl.BlockSpec((1, n_t, LANES), per_b),
        ],
        scratch_shapes=[
            pltpu.VMEM((2, n_pg, IDX_DIM, PAGE_SIZE), F32),
            pltpu.SemaphoreType.DMA((2, 1)),
        ],
    )
    return pl.pallas_call(
        functools.partial(_idx_sample_kernel, n_pg=n_pg),
        grid_spec=grid_spec,
        out_shape=[
            jax.ShapeDtypeStruct((bsz, n_t, n_pages * PAGE_SIZE), F32),
            jax.ShapeDtypeStruct((bsz, n_t, LANES), F32),
        ],
        compiler_params=_cparams(("arbitrary", "arbitrary")),
        name="idx_sample",
    )(page_table_flat, q_rows, w_rows, ik_new, kidx_pages)


def _thr_sample_kernel(score_ref, snew_ref, thr_ref, pcut_ref, key_scr, pcut_scr, *, n_t, past, topk):
    rows, width = key_scr.shape
    row0 = pl.program_id(0) * rows
    kpos = lax.broadcasted_iota(I32, (rows, width), 1)
    qpos = past + (row0 + lax.broadcasted_iota(I32, (rows, width), 0)) % n_t
    causal = kpos <= qpos
    key_scr[:, :past] = _sort_key(score_ref[...], causal[:, :past])
    key_scr[:, past:] = _sort_key(snew_ref[...], causal[:, past:])

    def count(pred):
        return jnp.sum(jnp.where(pred(key_scr[...], kpos), 1.0, 0.0), axis=1, keepdims=True)

    thr, pcut = _topk_threshold(count, (rows, 1), topk, pcut_scr, int(width - 1).bit_length())
    thr_ref[...] = thr
    pcut_ref[...] = pcut


def _thr_sample(score2d, snew2d, n_t, past, topk, rb):
    m = score2d.shape[0]
    width = past + LANES
    row = lambda i: (i, 0)
    return pl.pallas_call(
        functools.partial(_thr_sample_kernel, n_t=n_t, past=past, topk=topk),
        grid=(m // rb,),
        in_specs=[pl.BlockSpec((rb, past), row), pl.BlockSpec((rb, LANES), row)],
        out_specs=[pl.BlockSpec((rb, 1), row), pl.BlockSpec((rb, 1), row)],
        out_shape=[jax.ShapeDtypeStruct((m, 1), I32), jax.ShapeDtypeStruct((m, 1), I32)],
        scratch_shapes=[pltpu.VMEM((rb, width), I32), pltpu.VMEM((rb, 1), I32)],
        compiler_params=_cparams(("parallel",)),
        name="thr_sample",
    )(score2d, snew2d)


def _attn_sample_kernel(pt_ref, q_ref, score_ref, snew_ref, thr_ref, pcut_ref, knew_ref, vnew_ref,
                        k_hbm, v_hbm, o_ref, m_scr, l_scr, acc_scr, k_buf, v_buf, sems,
                        *, n_pg, n_t, past, n_new):
    slot = _fetch_pages(pt_ref, [k_hbm, v_hbm], [k_buf, v_buf], sems, n_pg)
    g = pl.program_id(1)
    rows = n_t * DSA_HEADS
    width = n_pg * PAGE_SIZE

    @pl.when(g == 0)
    def _():
        m_scr[...] = jnp.full(m_scr.shape, -jnp.inf, F32)
        l_scr[...] = jnp.zeros(l_scr.shape, F32)
        acc_scr[...] = jnp.zeros(acc_scr.shape, F32)

    thr = thr_ref[0]
    pcut = pcut_ref[0]
    qpos = past + lax.broadcasted_iota(I32, (n_t, 1), 0)

    def per_head(a):
        return jnp.broadcast_to(a[:, None, :], (n_t, DSA_HEADS, a.shape[1])).reshape(rows, a.shape[1])

    def bias_of(score, kpos):
        causal = kpos <= qpos
        sel = _select(_sort_key(score, causal), thr, pcut, kpos, causal)
        return per_head(jnp.where(sel, 0.0, -jnp.inf))

    head_of_row = lax.broadcasted_iota(I32, (rows, DSA_W), 0) % DSA_HEADS
    head_of_col = lax.broadcasted_iota(I32, (rows, DSA_W), 1) // DSA_HEAD_DIM
    diag = head_of_row == head_of_col
    qbd = jnp.where(diag, per_head(q_ref[0]), 0.0).astype(BF16)

    def update(logits, pv):
        m_old = m_scr[...]
        m_new = jnp.maximum(m_old, jnp.max(logits, axis=1, keepdims=True))
        m_safe = jnp.where(m_new == -jnp.inf, 0.0, m_new)
        e = jnp.exp(logits - m_safe)
        alpha = jnp.exp(m_old - m_safe)
        l_scr[...] = alpha * l_scr[...] + jnp.sum(e, axis=1, keepdims=True)
        acc_scr[...] = alpha * acc_scr[...] + pv(e.astype(BF16))
        m_scr[...] = m_new

    def page(buf, i):
        return buf[slot, i].reshape(DSA_W, PAGE_SIZE).astype(BF16)

    def pv_pages(e):
        acc = _dot_nt(e[:, :PAGE_SIZE], page(v_buf, 0))
        for i in range(1, n_pg):
            acc = acc + _dot_nt(e[:, i * PAGE_SIZE:(i + 1) * PAGE_SIZE], page(v_buf, i))
        return acc

    kpos = g * width + lax.broadcasted_iota(I32, (n_t, width), 1)
    logits = jnp.concatenate([_dot(qbd, page(k_buf, i)) for i in range(n_pg)], axis=1)
    update(logits + bias_of(score_ref[0], kpos), pv_pages)

    @pl.when(g == pl.num_programs(1) - 1)
    def _():
        kpos_new = past + lax.broadcasted_iota(I32, (n_t, n_new), 1)
        bias_new = bias_of(snew_ref[0][:, :n_new], kpos_new)
        update(_dot_nt(qbd, knew_ref[0]) + bias_new, lambda e: _dot(e, vnew_ref[0]))
        o = jnp.where(diag, acc_scr[...] / l_scr[...], 0.0)
        o_ref[0] = jnp.sum(o.reshape(n_t, DSA_HEADS, DSA_W), axis=1)


def _attn_sample(page_table_flat, q, score, snew, thr, pcut, knew, vnew, k_pages, v_pages,
                 n_t, n_pages, n_pg, past):
    bsz = q.shape[0]
    ngrp = n_pages // n_pg
    n_new = knew.shape[1]
    rows = n_t * DSA_HEADS
    per_b = lambda b, g, pt: (b, 0, 0)
    page_shape = (DSA_HEADS, DSA_HEAD_DIM, PAGE_SIZE)
    grid_spec = pltpu.PrefetchScalarGridSpec(
        num_scalar_prefetch=1,
        grid=(bsz, ngrp),
        in_specs=[
            pl.BlockSpec((1, n_t, DSA_W), per_b),
            pl.BlockSpec((1, n_t, n_pg * PAGE_SIZE), lambda b, g, pt: (b, 0, g)),
            pl.BlockSpec((1, n_t, LANES), per_b),
            pl.BlockSpec((1, n_t, 1), per_b),
            pl.BlockSpec((1, n_t, 1), per_b),
            pl.BlockSpec((1, n_new, DSA_W), per_b),
            pl.BlockSpec((1, n_new, DSA_W), per_b),
            pl.BlockSpec(memory_space=pl.ANY),
            pl.BlockSpec(memory_space=pl.ANY),
        ],
        out_specs=pl.BlockSpec((1, n_t, DSA_W), per_b),
        scratch_shapes=[
            pltpu.VMEM((rows, 1), F32),
            pltpu.VMEM((rows, 1), F32),
            pltpu.VMEM((rows, DSA_W), F32),
            pltpu.VMEM((2, n_pg) + page_shape, F32),
            pltpu.VMEM((2, n_pg) + page_shape, F32),
            pltpu.SemaphoreType.DMA((2, 2)),
        ],
    )
    return pl.pallas_call(
        functools.partial(_attn_sample_kernel, n_pg=n_pg, n_t=n_t, past=past, n_new=n_new),
        grid_spec=grid_spec,
        out_shape=jax.ShapeDtypeStruct((bsz, n_t, DSA_W), F32),
        compiler_params=_cparams(("arbitrary", "arbitrary")),
        name="attn_sample",
    )(page_table_flat, q, score, snew, thr, pcut, knew, vnew, k_pages, v_pages)


def _layer_norm(x, g, b):
    mu = jnp.mean(x, axis=-1, keepdims=True)
    xc = x - mu
    var = jnp.mean(xc * xc, axis=-1, keepdims=True)
    return xc * lax.rsqrt(var + NORM_EPS) * g + b


def _post_kernel(x_ref, ga_ref, ds_ref, wo_ref, l1g_ref, l1b_ref, wg_ref, wu_ref, wd_ref,
                 l2g_ref, l2b_ref, y_ref, *, alpha):
    mix = _dot(ga_ref[...], wo_ref[:GV_W, :]) + _dot(ds_ref[...], wo_ref[GV_W:, :])
    h = _layer_norm(alpha * x_ref[...] + mix, l1g_ref[...], l1b_ref[...])
    hb = h.astype(BF16)
    a = _dot(hb, wg_ref[...])
    u = _dot(hb, wu_ref[...])
    f = _dot((a * jax.nn.sigmoid(a) * u).astype(BF16), wd_ref[...])
    y_ref[...] = _layer_norm(alpha * h + f, l2g_ref[...], l2b_ref[...])


def _post(x2d, gated, dsa, wo, l1g, l1b, wg, wu, wd, l2g, l2b, alpha, tm):
    m, d = x2d.shape
    dff = wg.shape[1]
    tm = min(tm, m)
    assert m % tm == 0
    row = lambda i: (i, 0)
    return pl.pallas_call(
        functools.partial(_post_kernel, alpha=alpha),
        grid=(m // tm,),
        in_specs=[
            pl.BlockSpec((tm, d), row),
            pl.BlockSpec((tm, GV_W), row),
            pl.BlockSpec((tm, DSA_W), row),
            _const_spec((GV_W + DSA_W, d)),
            _const_spec((1, d)),
            _const_spec((1, d)),
            _const_spec((d, dff)),
            _const_spec((d, dff)),
            _const_spec((dff, d)),
            _const_spec((1, d)),
            _const_spec((1, d)),
        ],
        out_specs=pl.BlockSpec((tm, d), row),
        out_shape=jax.ShapeDtypeStruct((m, d), F32),
        compiler_params=_cparams(("parallel",)),
        name="post",
    )(x2d, gated, dsa, wo, l1g, l1b, wg, wu, wd, l2g, l2b)


def _layer_weights(l, w_in, w_lr_up, b_gate, gla_norm_g, w_o, ln1_g, ln1_b,
                   w_ffn_gate, w_ffn_up, w_ffn_down, ln2_g, ln2_b):
    row = lambda a: a[l][None, :].astype(F32)
    w, wvt = _prep_w_in(w_in[l])
    return dict(
        w=w, wvt=wvt, wlr=_prep_w_lr(w_lr_up[l]), bg=row(b_gate), gn=row(gla_norm_g),
        wo=w_o[l].astype(BF16), l1g=row(ln1_g), l1b=row(ln1_b),
        wg=w_ffn_gate[l].astype(BF16), wu=w_ffn_up[l].astype(BF16), wd=w_ffn_down[l].astype(BF16),
        l2g=row(ln2_g), l2b=row(ln2_b))


def _prompt_layer(x, wts, alpha):
    bsz, length, d = x.shape
    tm = 512 if length % 512 == 0 else length
    tabs = _rope_tables(jnp.arange(length, dtype=F32))
    x2d = x.reshape(bsz * length, d)
    (gq, gk, gv, gr, gg, dq, dkb, iq, ikb, kt, vt, vtb, ikt, iwt) = _project_fm(
        x2d, wts["w"], wts["wvt"], wts["wlr"], wts["bg"], *tabs, tm, bsz)
    r3 = lambda a: a.reshape(bsz, length, a.shape[-1])
    chunk = int(np.gcd(length, GLA_CHUNK))
    n_sub = 4 if length % (4 * chunk) == 0 else 1
    s0 = jnp.zeros((bsz, GLA_HEADS, GLA_DK, GLA_DV), F32)
    gated, state = _gla(r3(gq), r3(gk), r3(gg), r3(gv), r3(gr), s0, wts["gn"], chunk, n_sub,
                        2 if bsz % 2 == 0 else 1)
    kt_tile = 256 if length % 256 == 0 else 128
    dsa = _dsa_prompt(r3(dq), r3(iq), iwt, r3(dkb), vtb, r3(ikb), 128, kt_tile)
    y = _post(x2d, gated.reshape(bsz * length, GV_W), dsa.reshape(bsz * length, DSA_W),
              wts["wo"], wts["l1g"], wts["l1b"], wts["wg"], wts["wu"], wts["wd"],
              wts["l2g"], wts["l2b"], alpha, 256)
    heads = lambda a: jnp.transpose(a.reshape(bsz, DSA_HEADS, DSA_HEAD_DIM, length), (0, 3, 1, 2))
    return (y.reshape(bsz, length, d), heads(kt), heads(vt), jnp.transpose(ikt, (0, 2, 1)), state)


def _sample_layer(x, wts, alpha, cache_k, cache_v, cache_kidx, state, page_table):
    bsz, n_t, d = x.shape
    n_pages = page_table.shape[1]
    past = n_pages * PAGE_SIZE
    m = bsz * n_t
    pos = past + jnp.arange(n_t, dtype=F32)
    tabs = [jnp.tile(t, (bsz, 1)) for t in _rope_tables(pos)]
    x2d = x.reshape(m, d)
    (gq, gk, gv, gr, gg, dq, dkb, iq, ikb, dk, dv, dvb, ik, iw) = _project(
        x2d, wts["w"], wts["wlr"], wts["bg"], *tabs, m)
    chunk = 16
    n_seq = 8 if bsz % 8 == 0 else 1
    pad3 = lambda a: jnp.pad(a.reshape(bsz, n_t, a.shape[-1]), ((0, 0), (0, chunk - n_t), (0, 0)))
    gated, state_new = _gla(pad3(gq), pad3(gk), pad3(gg), pad3(gv), pad3(gr), state, wts["gn"],
                            chunk, 1, n_seq)
    gated = gated[:, :n_t].reshape(m, GV_W)
    topk = min(IDX_TOPK, (past + n_t) // 4)
    pick = lambda want: max(c for c in (1, 2, 4, 8, 16, 32, 64) if c <= want and n_pages % c == 0)
    pt_flat = page_table.reshape(-1)
    q_rows = iq.reshape(bsz, n_t * IDX_HEADS, IDX_DIM)
    w_rows = iw.reshape(bsz, n_t * IDX_HEADS, 1)
    ik_new = jnp.pad(ikb[:, :IDX_DIM].reshape(bsz, n_t, IDX_DIM), ((0, 0), (0, LANES - n_t), (0, 0)))
    score, snew = _idx_sample(pt_flat, q_rows, w_rows, ik_new, cache_kidx, n_t, n_pages, pick(64))
    thr, pcut = _thr_sample(score.reshape(m, past), snew.reshape(m, LANES), n_t, past, topk,
                            64 if m % 64 == 0 else 32)
    n_new = 16
    new3 = lambda a: jnp.pad(a.reshape(bsz, n_t, DSA_W), ((0, 0), (0, n_new - n_t), (0, 0)))
    dsa = _attn_sample(pt_flat, dq.astype(F32).reshape(bsz, n_t, DSA_W), score, snew,
                       thr.reshape(bsz, n_t, 1), pcut.reshape(bsz, n_t, 1), new3(dkb), new3(dvb),
                       cache_k, cache_v, n_t, n_pages, pick(16), past)
    y = _post(x2d, gated, dsa.reshape(m, DSA_W).astype(BF16),
              wts["wo"], wts["l1g"], wts["l1b"], wts["wg"], wts["wu"], wts["wd"],
              wts["l2g"], wts["l2b"], alpha, 256)
    return (y.reshape(bsz, n_t, d),
            dk.reshape(bsz, n_t, DSA_HEADS, DSA_HEAD_DIM),
            dv.reshape(bsz, n_t, DSA_HEADS, DSA_HEAD_DIM),
            ik.reshape(bsz, n_t, IDX_DIM), state_new)


def kernel(x_prompt, x_sample, cache_k, cache_v, cache_kidx, state_gla, page_table, w_in, w_lr_up, b_gate, gla_norm_g, w_o, ln1_g, ln1_b, w_ffn_gate, w_ffn_up, w_ffn_down, ln2_g, ln2_b):
    depth = w_in.shape[0]
    alpha = float((2 * depth) ** 0.25)
    n_pool = cache_k.shape[1]
    xp, xs = x_prompt, x_sample
    outs_p = [[] for _ in range(4)]
    outs_s = [[] for _ in range(4)]
    for l in range(depth):
        wts = _layer_weights(l, w_in, w_lr_up, b_gate, gla_norm_g, w_o, ln1_g, ln1_b,
                             w_ffn_gate, w_ffn_up, w_ffn_down, ln2_g, ln2_b)
        xp, *rest_p = _prompt_layer(xp, wts, alpha)
        feat = lambda c: jnp.transpose(c, (0, 1, 3, 4, 2)).reshape(
            depth * n_pool, DSA_HEADS, DSA_HEAD_DIM, PAGE_SIZE)
        xs, *rest_s = _sample_layer(
            xs, wts, alpha, feat(cache_k), feat(cache_v),
            jnp.transpose(cache_kidx, (0, 1, 3, 2)).reshape(depth * n_pool, IDX_DIM, PAGE_SIZE),
            state_gla[l], page_table + l * n_pool)
        for acc, val in zip(outs_p, rest_p):
            acc.append(val)
        for acc, val in zip(outs_s, rest_s):
            acc.append(val)
    stack = lambda seq: jnp.stack(seq)
    return (xp, xs, *[stack(a) for a in outs_p], *[stack(a) for a in outs_s])
```

```python
import functools

import numpy as np
import jax
import jax.numpy as jnp
from jax import lax
from jax.experimental import pallas as pl
from jax.experimental.pallas import tpu as pltpu

F32 = jnp.float32
BF16 = jnp.bfloat16
I32 = jnp.int32

GLA_HEADS = 4
GLA_DK = 64
GLA_DV = 128
GLA_LR_RANK = 16
GATE_TAU = 16.0
GLA_CHUNK = 64
EXP_CLAMP = 80.0
DSA_HEADS = 8
DSA_HEAD_DIM = 64
ROT_DIM = 16
IDX_HEADS = 8
IDX_DIM = 64
IDX_TOPK = 256
PAGE_SIZE = 128
ROPE_THETA = 500000.0
NORM_EPS = 1e-5

LANES = 128
SUBLANES = 8
ROW_CHUNK = 64
INT_MIN = int(np.iinfo(np.int32).min)
INT_MAX = int(np.iinfo(np.int32).max)
VMEM_LIMIT = 56 * 1024 * 1024

GQ_W = GLA_HEADS * GLA_DK
GV_W = GLA_HEADS * GLA_DV
DSA_W = DSA_HEADS * DSA_HEAD_DIM
IDX_W = IDX_HEADS * IDX_DIM
OFF_GQ = 0
OFF_GK = OFF_GQ + GQ_W
OFF_GV = OFF_GK + GQ_W
OFF_GR = OFF_GV + GV_W
OFF_DQ = OFF_GR + GV_W
OFF_DK = OFF_DQ + DSA_W
OFF_DV = OFF_DK + DSA_W
OFF_IQ = OFF_DV + DSA_W
OFF_MISC = OFF_IQ + IDX_W
N_COLS = OFF_MISC + LANES
MISC_GLR = IDX_DIM
MISC_IW = IDX_DIM + GLA_LR_RANK

NT_DIMS = (((1,), (1,)), ((), ()))
TN_DIMS = (((0,), (0,)), ((), ()))


def _dot(a, b):
    return jnp.dot(a, b, preferred_element_type=F32)


def _dot_nt(a, b):
    return lax.dot_general(a, b, NT_DIMS, preferred_element_type=F32)


def _dot_tn(a, b):
    return lax.dot_general(a, b, TN_DIMS, preferred_element_type=F32)


def _cparams(sem):
    return pltpu.CompilerParams(dimension_semantics=sem, vmem_limit_bytes=VMEM_LIMIT)


def _const_spec(shape):
    nd = len(shape)
    return pl.BlockSpec(shape, lambda *_: (0,) * nd, pipeline_mode=pl.Buffered(1))


def _proj_shared(x_ref, w_ref, wlr_ref, bg_ref, cos_ref, sina_ref, sinb_ref,
                 gq_ref, gk_ref, gv_ref, gr_ref, gg_ref, dq_ref, dkb_ref, iq_ref, ikb_ref):
    xb = x_ref[...].astype(BF16)
    cosp = cos_ref[...]
    sina = sina_ref[...]
    sinb = sinb_ref[...]

    def seg(off, width):
        return _dot(xb, w_ref[:, off:off + width])

    def rope(u):
        return u * cosp + pltpu.roll(u, LANES - ROT_DIM // 2, 1) * sina + pltpu.roll(u, ROT_DIM // 2, 1) * sinb

    gq_ref[...] = seg(OFF_GQ, GQ_W) * (GLA_DK ** -0.5)
    gk_ref[...] = seg(OFF_GK, GQ_W)
    gv_ref[...] = seg(OFF_GV, GV_W).astype(BF16)
    gr_ref[...] = seg(OFF_GR, GV_W).astype(BF16)
    uq = seg(OFF_DQ, DSA_W)
    uk = seg(OFF_DK, DSA_W)
    ui = seg(OFF_IQ, IDX_W)
    keys = []
    for i in range(DSA_W // LANES):
        sl = slice(i * LANES, (i + 1) * LANES)
        dq_ref[:, sl] = (rope(uq[:, sl]) * (DSA_HEAD_DIM ** -0.5)).astype(BF16)
        k = rope(uk[:, sl])
        dkb_ref[:, sl] = k.astype(BF16)
        keys.append(k)
        iq_ref[:, sl] = (rope(ui[:, sl]) * (IDX_DIM ** -0.5)).astype(BF16)
    misc = seg(OFF_MISC, LANES)
    mr = rope(misc)
    lane = lax.broadcasted_iota(I32, (1, LANES), 1)
    ikb_ref[...] = jnp.where(lane < IDX_DIM, mr, pltpu.roll(mr, IDX_DIM, 1)).astype(BF16)
    z = _dot(misc.astype(BF16), wlr_ref[...]) + bg_ref[...]
    gg_ref[...] = (jnp.minimum(z, 0.0) - jnp.log1p(jnp.exp(-jnp.abs(z)))) * (1.0 / GATE_TAU)
    return xb, keys, misc, mr


def _proj_kernel(x_ref, w_ref, wlr_ref, bg_ref, cos_ref, sina_ref, sinb_ref,
                 gq_ref, gk_ref, gv_ref, gr_ref, gg_ref, dq_ref, dkb_ref, iq_ref, ikb_ref,
                 dk_ref, dv_ref, dvb_ref, ik_ref, iw_ref):
    xb, keys, misc, mr = _proj_shared(x_ref, w_ref, wlr_ref, bg_ref, cos_ref, sina_ref, sinb_ref,
                                      gq_ref, gk_ref, gv_ref, gr_ref, gg_ref, dq_ref, dkb_ref, iq_ref, ikb_ref)
    for i, k in enumerate(keys):
        dk_ref[:, i * LANES:(i + 1) * LANES] = k
    v = _dot(xb, w_ref[:, OFF_DV:OFF_DV + DSA_W])
    dv_ref[...] = v
    dvb_ref[...] = v.astype(BF16)
    ik_ref[...] = mr[:, :IDX_DIM]
    iw_ref[...] = misc[:, MISC_IW:MISC_IW + IDX_HEADS] * (IDX_HEADS ** -0.5)


def _proj_fm_kernel(x_ref, w_ref, wvt_ref, wlr_ref, bg_ref, cos_ref, sina_ref, sinb_ref,
                    gq_ref, gk_ref, gv_ref, gr_ref, gg_ref, dq_ref, dkb_ref, iq_ref, ikb_ref,
                    kt_ref, vt_ref, vtb_ref, ikt_ref, iwt_ref):
    xb, keys, misc, mr = _proj_shared(x_ref, w_ref, wlr_ref, bg_ref, cos_ref, sina_ref, sinb_ref,
                                      gq_ref, gk_ref, gv_ref, gr_ref, gg_ref, dq_ref, dkb_ref, iq_ref, ikb_ref)
    for i, k in enumerate(keys):
        kt_ref[0, i * LANES:(i + 1) * LANES, :] = k.T
    vt = _dot_nt(wvt_ref[...], xb)
    vt_ref[0] = vt
    vtb_ref[0] = vt.astype(BF16)
    mt = mr.T
    ikt_ref[0] = mt[:IDX_DIM]
    iwt_ref[0] = mt[MISC_IW:MISC_IW + IDX_HEADS] * (IDX_HEADS ** -0.5)


_PROJ_SHARED_OUTS = [
    (GQ_W, F32), (GQ_W, F32), (GV_W, BF16), (GV_W, BF16), (GQ_W, F32),
    (DSA_W, BF16), (DSA_W, BF16), (IDX_W, BF16), (LANES, BF16),
]


def _project(x2d, w, wlr, bg, cos_t, sina_t, sinb_t, tm):
    m, d = x2d.shape
    period = cos_t.shape[0]
    assert m % tm == 0 and period % tm == 0
    nper = period // tm
    row = lambda i: (i, 0)
    tab = lambda i: (i % nper, 0)
    outs = _PROJ_SHARED_OUTS + [(DSA_W, F32), (DSA_W, F32), (DSA_W, BF16), (IDX_DIM, F32), (IDX_HEADS, F32)]
    return pl.pallas_call(
        _proj_kernel,
        grid=(m // tm,),
        in_specs=[
            pl.BlockSpec((tm, d), row),
            _const_spec((d, N_COLS)),
            _const_spec((LANES, GQ_W)),
            _const_spec((1, GQ_W)),
            pl.BlockSpec((tm, LANES), tab),
            pl.BlockSpec((tm, LANES), tab),
            pl.BlockSpec((tm, LANES), tab),
        ],
        out_specs=[pl.BlockSpec((tm, wd), row) for wd, _ in outs],
        out_shape=[jax.ShapeDtypeStruct((m, wd), dt) for wd, dt in outs],
        compiler_params=_cparams(("parallel",)),
        name="project",
    )(x2d, w, wlr, bg, cos_t, sina_t, sinb_t)


def _project_fm(x2d, w, wvt, wlr, bg, cos_t, sina_t, sinb_t, tm, bsz):
    m, d = x2d.shape
    length = m // bsz
    assert length % tm == 0 and cos_t.shape[0] == length
    nper = length // tm
    row = lambda i: (i, 0)
    tab = lambda i: (i % nper, 0)
    fm = lambda i: (i // nper, 0, i % nper)
    fm_outs = [(DSA_W, F32), (DSA_W, F32), (DSA_W, BF16), (IDX_DIM, F32), (IDX_HEADS, F32)]
    return pl.pallas_call(
        _proj_fm_kernel,
        grid=(m // tm,),
        in_specs=[
            pl.BlockSpec((tm, d), row),
            _const_spec((d, N_COLS)),
            _const_spec((DSA_W, d)),
            _const_spec((LANES, GQ_W)),
            _const_spec((1, GQ_W)),
            pl.BlockSpec((tm, LANES), tab),
            pl.BlockSpec((tm, LANES), tab),
            pl.BlockSpec((tm, LANES), tab),
        ],
        out_specs=[pl.BlockSpec((tm, wd), row) for wd, _ in _PROJ_SHARED_OUTS]
        + [pl.BlockSpec((1, wd, tm), fm) for wd, _ in fm_outs],
        out_shape=[jax.ShapeDtypeStruct((m, wd), dt) for wd, dt in _PROJ_SHARED_OUTS]
        + [jax.ShapeDtypeStruct((bsz, wd, length), dt) for wd, dt in fm_outs],
        compiler_params=_cparams(("parallel",)),
        name="project_fm",
    )(x2d, w, wvt, wlr, bg, cos_t, sina_t, sinb_t)


_IN_SIZES = (GQ_W, GQ_W, GV_W, GV_W, GLA_LR_RANK, DSA_W, DSA_W, DSA_W, IDX_W, IDX_DIM, IDX_HEADS)


def _prep_w_in(w_in):
    cuts = np.cumsum(_IN_SIZES)[:-1].tolist()
    gq, gk, gv, gr, glr, dq, dk, dv, iq, ik, iw = jnp.split(w_in, cuts, axis=1)
    pad = jnp.zeros((w_in.shape[0], LANES - IDX_DIM - GLA_LR_RANK - IDX_HEADS), w_in.dtype)
    w = jnp.concatenate([gq, gk, gv, gr, dq, dk, dv, iq, ik, glr, iw, pad], axis=1).astype(BF16)
    return w, jnp.transpose(dv).astype(BF16)


def _prep_w_lr(w_lr_up):
    z0 = jnp.zeros((MISC_GLR, GQ_W), w_lr_up.dtype)
    z1 = jnp.zeros((LANES - MISC_GLR - GLA_LR_RANK, GQ_W), w_lr_up.dtype)
    return jnp.concatenate([z0, w_lr_up, z1], axis=0).astype(BF16)


def _rope_tables(pos):
    half = ROT_DIM // 2
    inv = jnp.power(ROPE_THETA, -jnp.arange(half, dtype=F32) * (2.0 / ROT_DIM))
    ang = pos[:, None] * inv[None, :]
    cos = jnp.cos(ang)
    sin = jnp.sin(ang)
    n = pos.shape[0]
    rest = DSA_HEAD_DIM - ROT_DIM
    z8 = jnp.zeros((n, half), F32)
    zr = jnp.zeros((n, rest), F32)
    cos_h = jnp.concatenate([cos, cos, jnp.ones((n, rest), F32)], axis=1)
    sina_h = jnp.concatenate([-sin, z8, zr], axis=1)
    sinb_h = jnp.concatenate([z8, sin, zr], axis=1)
    rep = LANES // DSA_HEAD_DIM
    return jnp.tile(cos_h, (1, rep)), jnp.tile(sina_h, (1, rep)), jnp.tile(sinb_h, (1, rep))


def _gla_kernel(gq_ref, gk_ref, gg_ref, gv_ref, gr_ref, s0_ref, gn_ref, o_ref, sout_ref, st_scr,
                *, chunk, n_sub, n_seq):
    j = pl.program_id(1)
    lane = lax.broadcasted_iota(I32, (1, LANES), 1)
    lo = lane < GLA_DK

    @pl.when(j == 0)
    def _():
        zero = jnp.zeros((GLA_DK, GLA_DV), F32)
        for s in range(n_seq):
            for h in range(GLA_HEADS):
                s0 = s0_ref[s, h]
                full = jnp.concatenate([s0, zero] if h % 2 == 0 else [zero, s0], axis=0)
                st_scr[s * GLA_HEADS + h] = full.T

    r = lax.broadcasted_iota(I32, (chunk, chunk), 0)
    c = lax.broadcasted_iota(I32, (chunk, chunk), 1)
    tri = r >= c
    trib = jnp.where(tri, 1.0, 0.0).astype(BF16)
    gn = gn_ref[...]
    parts = []
    for s in range(n_seq):
        for ci in range(n_sub):
            rows = slice(ci * chunk, (ci + 1) * chunk)
            g = gg_ref[s, rows, :]
            g_hi = g.astype(BF16)
            g_lo = (g - g_hi.astype(F32)).astype(BF16)
            b = _dot(trib, g_hi) + _dot(trib, g_lo)
            b_last = b[chunk - 1:chunk, :]
            gq = gq_ref[s, rows, :]
            q = gq * jnp.exp(b)
            k = gk_ref[s, rows, :]
            k_out = k * jnp.exp(b_last - b)
            d_last = jnp.exp(b_last)
            ref = b[chunk // 2 - 1:chunk // 2, :]
            q_in = gq * jnp.exp(jnp.minimum(b - ref, EXP_CLAMP))
            k_in = k * jnp.exp(jnp.minimum(ref - b, EXP_CLAMP))
            for h in range(GLA_HEADS):
                p, hh = divmod(h, 2)
                sl = slice(p * LANES, (p + 1) * LANES)
                msk = lo if hh == 0 else jnp.logical_not(lo)
                vh = gv_ref[s, rows, h * GLA_DV:(h + 1) * GLA_DV]
                a = _dot_nt(jnp.where(msk, q_in[:, sl], 0.0).astype(BF16), k_in[:, sl].astype(BF16))
                a = jnp.where(tri, a, 0.0).astype(BF16)
                qm = jnp.where(msk, q[:, sl], 0.0).astype(BF16)
                km = jnp.where(msk, k_out[:, sl], 0.0).astype(BF16)
                parts.append((qm, _dot(a, vh), _dot_tn(vh, km), d_last[:, sl]))
    it = iter(parts)
    for s in range(n_seq):
        for ci in range(n_sub):
            rows = slice(ci * chunk, (ci + 1) * chunk)
            for h in range(GLA_HEADS):
                qm, o_intra, kv, decay = next(it)
                vsl = slice(h * GLA_DV, (h + 1) * GLA_DV)
                st = st_scr[s * GLA_HEADS + h]
                o = o_intra + _dot_nt(qm, st.astype(BF16))
                st_scr[s * GLA_HEADS + h] = st * decay + kv
                ms = jnp.mean(o * o, axis=1, keepdims=True)
                of = o * lax.rsqrt(ms + NORM_EPS) * gn
                gr = gr_ref[s, rows, vsl].astype(F32)
                o_ref[s, rows, vsl] = (of * (gr * jax.nn.sigmoid(gr))).astype(BF16)

    @pl.when(j == pl.num_programs(1) - 1)
    def _():
        for s in range(n_seq):
            for h in range(GLA_HEADS):
                off = (h % 2) * GLA_DK
                sout_ref[s, h] = st_scr[s * GLA_HEADS + h].T[off:off + GLA_DK, :]


def _gla(gq, gk, gg, gv, gr, s0, gn, chunk, n_sub, n_seq):
    bsz, length, _ = gq.shape
    tl = chunk * n_sub
    assert length % tl == 0 and bsz % n_seq == 0
    tok = lambda b, j: (b, j, 0)
    st = lambda b, j: (b, 0, 0, 0)
    return pl.pallas_call(
        functools.partial(_gla_kernel, chunk=chunk, n_sub=n_sub, n_seq=n_seq),
        grid=(bsz // n_seq, length // tl),
        in_specs=[
            pl.BlockSpec((n_seq, tl, GQ_W), tok),
            pl.BlockSpec((n_seq, tl, GQ_W), tok),
            pl.BlockSpec((n_seq, tl, GQ_W), tok),
            pl.BlockSpec((n_seq, tl, GV_W), tok),
            pl.BlockSpec((n_seq, tl, GV_W), tok),
            pl.BlockSpec((n_seq, GLA_HEADS, GLA_DK, GLA_DV), st),
            pl.BlockSpec((1, GLA_DV), lambda b, j: (0, 0)),
        ],
        out_specs=[
            pl.BlockSpec((n_seq, tl, GV_W), tok),
            pl.BlockSpec((n_seq, GLA_HEADS, GLA_DK, GLA_DV), st),
        ],
        out_shape=[
            jax.ShapeDtypeStruct((bsz, length, GV_W), BF16),
            jax.ShapeDtypeStruct((bsz, GLA_HEADS, GLA_DK, GLA_DV), F32),
        ],
        scratch_shapes=[pltpu.VMEM((n_seq * GLA_HEADS, GLA_DV, LANES), F32)],
        compiler_params=_cparams(("parallel", "arbitrary")),
        name="gla",
    )(gq, gk, gg, gv, gr, s0, gn)


def _mask_scores(score, causal):
    return jnp.where(causal, jnp.where(score == 0.0, 0.0, score), -jnp.inf)


def _ordered_float(key):
    return pltpu.bitcast(key ^ ((key >> 31) & INT_MAX), F32)


def _topk_threshold(count, shape, topk, pcut_ref, idx_bits):
    kf = float(topk)
    cnt0 = count(lambda sc, kpos: sc >= 0.0)
    key0 = jnp.where(cnt0 >= kf, 0, INT_MIN).astype(I32)

    def body(i, key):
        cand = key + jnp.left_shift(jnp.int32(1), 30 - i)
        cand_f = _ordered_float(cand)
        cnt = count(lambda sc, kpos: sc >= cand_f)
        return jnp.where(cnt >= kf, cand, key)

    thr = _ordered_float(lax.fori_loop(0, 31, body, key0))
    need = kf - count(lambda sc, kpos: sc > thr)
    n_eq = count(lambda sc, kpos: sc == thr)
    cut = n_eq > need
    pcut_ref[...] = jnp.full(shape, INT_MAX, I32)

    @pl.when(jnp.max(jnp.where(cut, 1.0, 0.0)) > 0.0)
    def _():
        def pbody(i, pos):
            cand = pos + jnp.left_shift(jnp.int32(1), idx_bits - 1 - i)
            cnt = count(lambda sc, kpos: jnp.logical_and(sc == thr, kpos < cand))
            return jnp.where(cnt < need, cand, pos)

        pos = lax.fori_loop(0, idx_bits, pbody, jnp.zeros(shape, I32))
        pcut_ref[...] = jnp.where(cut, pos, INT_MAX)

    return thr, pcut_ref[...]


def _select(score, thr, pcut, kpos, qpos, topk):
    keep = jnp.logical_or(score > thr, jnp.logical_and(score == thr, kpos <= pcut))
    keep = jnp.logical_or(keep, qpos < topk)
    return jnp.logical_and(keep, kpos <= qpos)


def _dsa_kernel(dq_ref, iq_ref, iwt_ref, k_ref, vt_ref, ik_ref, o_ref,
                key_scr, bias_scr, logit_scr, p_scr, out_scr, pcut_scr, *, tq, kt, nkt, qb0, topk, idx_bits):
    j = pl.program_id(1)
    t0 = (qb0 + j) * tq
    lane = lax.broadcasted_iota(I32, (1, LANES), 1)
    lo = lane < DSA_HEAD_DIM
    qpos = t0 + lax.broadcasted_iota(I32, (1, tq), 1)
    wt = iwt_ref[0]

    def head_pair(ref, p):
        qs = ref[0, :, p * LANES:(p + 1) * LANES]
        zero = jnp.zeros_like(qs)
        return jnp.concatenate([jnp.where(lo, qs, zero), jnp.where(lo, zero, qs)], axis=0)

    def rows_of(c):
        return pl.ds(pl.multiple_of(c * kt, kt), kt)

    def kpos_of(c):
        return c * kt + lax.broadcasted_iota(I32, (kt, tq), 0)

    def col_sum(x):
        return jnp.sum(x.reshape(kt // SUBLANES, SUBLANES, tq), axis=0)

    n_pair = IDX_HEADS // 2
    iq2 = [head_pair(iq_ref, p) for p in range(n_pair)]
    w2 = [jnp.concatenate([wt[2 * p:2 * p + 1, :], wt[2 * p + 1:2 * p + 2, :]], axis=1) for p in range(n_pair)]

    def score_tile(c, carry):
        rs = pl.ds(pl.multiple_of(c * tq, tq), tq)
        ik_t = ik_ref[0, rs, :]
        acc2 = jnp.zeros((tq, 2 * tq), F32)
        for p in range(n_pair):
            acc2 = acc2 + jnp.maximum(_dot_nt(ik_t, iq2[p]), 0.0) * w2[p]
        kpos = c * tq + lax.broadcasted_iota(I32, (tq, tq), 0)
        key_scr[rs, :] = _mask_scores(acc2[:, :tq] + acc2[:, tq:], kpos <= qpos)
        return carry

    lax.fori_loop(0, nkt * kt // tq, score_tile, 0, unroll=4 if (nkt * kt // tq) % 4 == 0 else 2)

    def count(pred):
        acc = jnp.zeros((SUBLANES, tq), F32)
        for c in range(nkt):
            hit = pred(key_scr[c * kt:(c + 1) * kt, :], kpos_of(c))
            acc = acc + col_sum(jnp.where(hit, 1.0, 0.0))
        return jnp.sum(acc, axis=0, keepdims=True)

    thr, pcut = _topk_threshold(count, (1, tq), topk, pcut_scr, idx_bits)

    def bias_tile(c, carry):
        sel = _select(key_scr[rows_of(c), :], thr, pcut, kpos_of(c), qpos, topk)
        bias_scr[rows_of(c), :] = jnp.where(sel, 0.0, -jnp.inf)
        return carry

    lax.fori_loop(0, nkt, bias_tile, 0)

    extent = nkt * kt
    n_chunk = extent // ROW_CHUNK
    n_pair = DSA_HEADS // 2
    q2s = [head_pair(dq_ref, p) for p in range(n_pair)]

    def chunk_rows(i):
        return pl.ds(pl.multiple_of(i * ROW_CHUNK, ROW_CHUNK), ROW_CHUNK)

    def col_max(x):
        return jnp.max(x.reshape(kt // SUBLANES, SUBLANES, tq), axis=0)

    def logit_body(c, carry):
        rs = rows_of(c)
        b = bias_scr[rs, :]
        out = []
        for p in range(n_pair):
            x = _dot_nt(k_ref[0, rs, p * LANES:(p + 1) * LANES], q2s[p])
            xa = x[:, :tq] + b
            xb = x[:, tq:] + b
            logit_scr[rs, 2 * p * tq:(2 * p + 2) * tq] = jnp.concatenate([xa, xb], axis=1)
            out.append(jnp.maximum(carry[2 * p], col_max(xa)))
            out.append(jnp.maximum(carry[2 * p + 1], col_max(xb)))
        return tuple(out)

    neg = jnp.full((SUBLANES, tq), -jnp.inf, F32)
    m8 = lax.fori_loop(0, nkt, logit_body, (neg,) * DSA_HEADS, unroll=2 if nkt % 2 == 0 else 1)
    m_all = jnp.concatenate([jnp.max(m, axis=0, keepdims=True) for m in m8], axis=1)

    def exp_body(i, l8):
        e = jnp.exp(logit_scr[chunk_rows(i), :] - m_all)
        p_scr[chunk_rows(i), :] = e.astype(BF16)
        return l8 + jnp.sum(e.reshape(ROW_CHUNK // SUBLANES, SUBLANES, DSA_HEADS * tq), axis=0)

    l8 = lax.fori_loop(0, n_chunk, exp_body, jnp.zeros((SUBLANES, DSA_HEADS * tq), F32), unroll=2)
    l_all = jnp.sum(l8, axis=0, keepdims=True)
    for h in range(DSA_HEADS):
        vsl = slice(h * DSA_HEAD_DIM, (h + 1) * DSA_HEAD_DIM)
        qsl = slice(h * tq, (h + 1) * tq)
        out_scr[vsl, :] = _dot(vt_ref[0, vsl, :], p_scr[:, qsl]) / l_all[:, qsl]

    o_ref[0] = out_scr[...].T.astype(BF16)


def _dsa_prompt_bucket(dq, iq, iwt, dkb, vtb, ikb, tq, kt, qb0, nqb, topk, idx_bits):
    bsz = dq.shape[0]
    extent = (qb0 + nqb) * tq
    assert extent % kt == 0
    qblk = lambda b, j: (b, qb0 + j, 0)
    head = lambda b, j: (b, 0, 0)
    return pl.pallas_call(
        functools.partial(_dsa_kernel, tq=tq, kt=kt, nkt=extent // kt, qb0=qb0, topk=topk, idx_bits=idx_bits),
        grid=(bsz, nqb),
        in_specs=[
            pl.BlockSpec((1, tq, DSA_W), qblk),
            pl.BlockSpec((1, tq, IDX_W), qblk),
            pl.BlockSpec((1, IDX_HEADS, tq), lambda b, j: (b, 0, qb0 + j)),
            pl.BlockSpec((1, extent, DSA_W), head),
            pl.BlockSpec((1, DSA_W, extent), head),
            pl.BlockSpec((1, extent, LANES), head),
        ],
        out_specs=pl.BlockSpec((1, tq, DSA_W), lambda b, j: (b, j, 0)),
        out_shape=jax.ShapeDtypeStruct((bsz, nqb * tq, DSA_W), BF16),
        scratch_shapes=[
            pltpu.VMEM((extent, tq), F32),
            pltpu.VMEM((extent, tq), F32),
            pltpu.VMEM((extent, DSA_HEADS * tq), F32),
            pltpu.VMEM((extent, DSA_HEADS * tq), BF16),
            pltpu.VMEM((DSA_W, tq), F32),
            pltpu.VMEM((1, tq), I32),
        ],
        compiler_params=_cparams(("parallel", "arbitrary")),
        name="dsa_prompt",
    )(dq, iq, iwt, dkb, vtb, ikb)


def _dsa_prompt(dq, iq, iwt, dkb, vtb, ikb, tq, kt):
    length = dq.shape[1]
    topk = min(IDX_TOPK, length // 4)
    nqb = kt // tq
    assert length % kt == 0 and kt % tq == 0
    idx_bits = int(length - 1).bit_length()
    outs = [_dsa_prompt_bucket(dq, iq, iwt, dkb, vtb, ikb, tq, kt, qb0, nqb, topk, idx_bits)
            for qb0 in range(0, length // tq, nqb)]
    return jnp.concatenate(outs, axis=1)


def _fetch_pages(pt_ref, hbm_refs, bufs, sems, n_pg):
    ngrp = pl.num_programs(1)
    s = pl.program_id(0) * ngrp + pl.program_id(1)
    total = pl.num_programs(0) * ngrp
    slot = s % 2

    def copies(step, slot_):
        return [pltpu.make_async_copy(hbm.at[pt_ref[step * n_pg + i]], buf.at[slot_, i], sems.at[slot_, a])
                for a, (hbm, buf) in enumerate(zip(hbm_refs, bufs)) for i in range(n_pg)]

    @pl.when(s == 0)
    def _():
        for c in copies(0, 0):
            c.start()

    @pl.when(s + 1 < total)
    def _():
        for c in copies(s + 1, 1 - slot):
            c.start()

    for c in copies(s, slot):
        c.wait()
    return slot


def _idx_sample_kernel(pt_ref, q_ref, w_ref, iknew_ref, kidx_hbm, score_ref, snew_ref, page_buf, sems, *, n_pg):
    slot = _fetch_pages(pt_ref, [kidx_hbm], [page_buf], sems, n_pg)
    q = q_ref[0]
    w = w_ref[0]
    n_t = score_ref.shape[1]

    def head_sum(s):
        r = jnp.maximum(s, 0.0) * w
        return jnp.sum(r.reshape(n_t, IDX_HEADS, r.shape[1]), axis=1)

    for i in range(n_pg):
        score_ref[0, :, i * PAGE_SIZE:(i + 1) * PAGE_SIZE] = head_sum(_dot(q, page_buf[slot, i].astype(BF16)))

    @pl.when(pl.program_id(1) == 0)
    def _():
        snew_ref[0] = head_sum(_dot_nt(q, iknew_ref[0]))


def _idx_sample(page_table_flat, q_rows, w_rows, ik_new, kidx_pages, n_t, n_pages, n_pg):
    bsz = q_rows.shape[0]
    ngrp = n_pages // n_pg
    per_b = lambda b, g, pt: (b, 0, 0)
    grid_spec = pltpu.PrefetchScalarGridSpec(
        num_scalar_prefetch=1,
        grid=(bsz, ngrp),
        in_specs=[
            pl.BlockSpec((1, n_t * IDX_HEADS, IDX_DIM), per_b),
            pl.BlockSpec((1, n_t * IDX_HEADS, 1), per_b),
            pl.BlockSpec((1, LANES, IDX_DIM), per_b),
            pl.BlockSpec(memory_space=pl.ANY),
        ],
        out_specs=[
            pl.BlockSpec((1, n_t, n_pg * PAGE_SIZE), lambda b, g, pt: (b, 0, g)),
            pl.BlockSpec((1, n_t, LANES), per_b),
        ],
        scratch_shapes=[
            pltpu.VMEM((2, n_pg, IDX_DIM, PAGE_SIZE), F32),
            pltpu.SemaphoreType.DMA((2, 1)),
        ],
    )
    return pl.pallas_call(
        functools.partial(_idx_sample_kernel, n_pg=n_pg),
        grid_spec=grid_spec,
        out_shape=[
            jax.ShapeDtypeStruct((bsz, n_t, n_pages * PAGE_SIZE), F32),
            jax.ShapeDtypeStruct((bsz, n_t, LANES), F32),
        ],
        compiler_params=_cparams(("arbitrary", "arbitrary")),
        name="idx_sample",
    )(page_table_flat, q_rows, w_rows, ik_new, kidx_pages)


def _thr_sample_kernel(score_ref, snew_ref, thr_ref, pcut_ref, key_scr, pcut_scr, *, n_t, past, topk):
    rows, width = key_scr.shape
    row0 = pl.program_id(0) * rows
    kpos = lax.broadcasted_iota(I32, (rows, width), 1)
    qpos = past + (row0 + lax.broadcasted_iota(I32, (rows, width), 0)) % n_t
    causal = kpos <= qpos
    key_scr[:, :past] = _mask_scores(score_ref[...], causal[:, :past])
    key_scr[:, past:] = _mask_scores(snew_ref[...], causal[:, past:])

    def count(pred):
        return jnp.sum(jnp.where(pred(key_scr[...], kpos), 1.0, 0.0), axis=1, keepdims=True)

    thr, pcut = _topk_threshold(count, (rows, 1), topk, pcut_scr, int(width - 1).bit_length())
    thr_ref[...] = thr
    pcut_ref[...] = pcut


def _thr_sample(score2d, snew2d, n_t, past, topk, rb):
    m = score2d.shape[0]
    width = past + LANES
    row = lambda i: (i, 0)
    return pl.pallas_call(
        functools.partial(_thr_sample_kernel, n_t=n_t, past=past, topk=topk),
        grid=(m // rb,),
        in_specs=[pl.BlockSpec((rb, past), row), pl.BlockSpec((rb, LANES), row)],
        out_specs=[pl.BlockSpec((rb, 1), row), pl.BlockSpec((rb, 1), row)],
        out_shape=[jax.ShapeDtypeStruct((m, 1), F32), jax.ShapeDtypeStruct((m, 1), I32)],
        scratch_shapes=[pltpu.VMEM((rb, width), F32), pltpu.VMEM((rb, 1), I32)],
        compiler_params=_cparams(("parallel",)),
        name="thr_sample",
    )(score2d, snew2d)


def _attn_sample_kernel(pt_ref, q_ref, score_ref, snew_ref, thr_ref, pcut_ref, knew_ref, vnew_ref,
                        k_hbm, v_hbm, o_ref, m_scr, l_scr, acc_scr, k_buf, v_buf, sems,
                        *, n_pg, n_t, past, n_new, topk):
    slot = _fetch_pages(pt_ref, [k_hbm, v_hbm], [k_buf, v_buf], sems, n_pg)
    g = pl.program_id(1)
    rows = n_t * DSA_HEADS
    width = n_pg * PAGE_SIZE

    @pl.when(g == 0)
    def _():
        m_scr[...] = jnp.full(m_scr.shape, -jnp.inf, F32)
        l_scr[...] = jnp.zeros(l_scr.shape, F32)
        acc_scr[...] = jnp.zeros(acc_scr.shape, F32)

    thr = thr_ref[0]
    pcut = pcut_ref[0]
    qpos = past + lax.broadcasted_iota(I32, (n_t, 1), 0)

    def per_head(a):
        return jnp.broadcast_to(a[:, None, :], (n_t, DSA_HEADS, a.shape[1])).reshape(rows, a.shape[1])

    def bias_of(score, kpos):
        sel = _select(_mask_scores(score, kpos <= qpos), thr, pcut, kpos, qpos, topk)
        return per_head(jnp.where(sel, 0.0, -jnp.inf))

    head_of_row = lax.broadcasted_iota(I32, (rows, DSA_W), 0) % DSA_HEADS
    head_of_col = lax.broadcasted_iota(I32, (rows, DSA_W), 1) // DSA_HEAD_DIM
    diag = head_of_row == head_of_col
    qbd = jnp.where(diag, per_head(q_ref[0]), 0.0).astype(BF16)

    def update(logits, pv):
        m_old = m_scr[...]
        m_new = jnp.maximum(m_old, jnp.max(logits, axis=1, keepdims=True))
        m_safe = jnp.where(m_new == -jnp.inf, 0.0, m_new)
        e = jnp.exp(logits - m_safe)
        alpha = jnp.exp(m_old - m_safe)
        l_scr[...] = alpha * l_scr[...] + jnp.sum(e, axis=1, keepdims=True)
        acc_scr[...] = alpha * acc_scr[...] + pv(e.astype(BF16))
        m_scr[...] = m_new

    def page(buf, i):
        return buf[slot, i].reshape(DSA_W, PAGE_SIZE).astype(BF16)

    def pv_pages(e):
        acc = _dot_nt(e[:, :PAGE_SIZE], page(v_buf, 0))
        for i in range(1, n_pg):
            acc = acc + _dot_nt(e[:, i * PAGE_SIZE:(i + 1) * PAGE_SIZE], page(v_buf, i))
        return acc

    kpos = g * width + lax.broadcasted_iota(I32, (n_t, width), 1)
    logits = jnp.concatenate([_dot(qbd, page(k_buf, i)) for i in range(n_pg)], axis=1)
    update(logits + bias_of(score_ref[0], kpos), pv_pages)

    @pl.when(g == pl.num_programs(1) - 1)
    def _():
        kpos_new = past + lax.broadcasted_iota(I32, (n_t, n_new), 1)
        bias_new = bias_of(snew_ref[0][:, :n_new], kpos_new)
        update(_dot_nt(qbd, knew_ref[0]) + bias_new, lambda e: _dot(e, vnew_ref[0]))
        o = jnp.where(diag, acc_scr[...] / l_scr[...], 0.0)
        o_ref[0] = jnp.sum(o.reshape(n_t, DSA_HEADS, DSA_W), axis=1)


def _attn_sample(page_table_flat, q, score, snew, thr, pcut, knew, vnew, k_pages, v_pages,
                 n_t, n_pages, n_pg, past, topk):
    bsz = q.shape[0]
    ngrp = n_pages // n_pg
    n_new = knew.shape[1]
    rows = n_t * DSA_HEADS
    per_b = lambda b, g, pt: (b, 0, 0)
    page_shape = (DSA_HEADS, DSA_HEAD_DIM, PAGE_SIZE)
    grid_spec = pltpu.PrefetchScalarGridSpec(
        num_scalar_prefetch=1,
        grid=(bsz, ngrp),
        in_specs=[
            pl.BlockSpec((1, n_t, DSA_W), per_b),
            pl.BlockSpec((1, n_t, n_pg * PAGE_SIZE), lambda b, g, pt: (b, 0, g)),
            pl.BlockSpec((1, n_t, LANES), per_b),
            pl.BlockSpec((1, n_t, 1), per_b),
            pl.BlockSpec((1, n_t, 1), per_b),
            pl.BlockSpec((1, n_new, DSA_W), per_b),
            pl.BlockSpec((1, n_new, DSA_W), per_b),
            pl.BlockSpec(memory_space=pl.ANY),
            pl.BlockSpec(memory_space=pl.ANY),
        ],
        out_specs=pl.BlockSpec((1, n_t, DSA_W), per_b),
        scratch_shapes=[
            pltpu.VMEM((rows, 1), F32),
            pltpu.VMEM((rows, 1), F32),
            pltpu.VMEM((rows, DSA_W), F32),
            pltpu.VMEM((2, n_pg) + page_shape, F32),
            pltpu.VMEM((2, n_pg) + page_shape, F32),
            pltpu.SemaphoreType.DMA((2, 2)),
        ],
    )
    return pl.pallas_call(
        functools.partial(_attn_sample_kernel, n_pg=n_pg, n_t=n_t, past=past, n_new=n_new, topk=topk),
        grid_spec=grid_spec,
        out_shape=jax.ShapeDtypeStruct((bsz, n_t, DSA_W), F32),
        compiler_params=_cparams(("arbitrary", "arbitrary")),
        name="attn_sample",
    )(page_table_flat, q, score, snew, thr, pcut, knew, vnew, k_pages, v_pages)


def _layer_norm(x, g, b):
    mu = jnp.mean(x, axis=-1, keepdims=True)
    xc = x - mu
    var = jnp.mean(xc * xc, axis=-1, keepdims=True)
    return xc * lax.rsqrt(var + NORM_EPS) * g + b


def _post_kernel(x_ref, ga_ref, ds_ref, wo_ref, l1g_ref, l1b_ref, wg_ref, wu_ref, wd_ref,
                 l2g_ref, l2b_ref, y_ref, *, alpha):
    mix = _dot(ga_ref[...], wo_ref[:GV_W, :]) + _dot(ds_ref[...], wo_ref[GV_W:, :])
    h = _layer_norm(alpha * x_ref[...] + mix, l1g_ref[...], l1b_ref[...])
    hb = h.astype(BF16)
    a = _dot(hb, wg_ref[...])
    u = _dot(hb, wu_ref[...])
    f = _dot((a * jax.nn.sigmoid(a) * u).astype(BF16), wd_ref[...])
    y_ref[...] = _layer_norm(alpha * h + f, l2g_ref[...], l2b_ref[...])


def _post(x2d, gated, dsa, wo, l1g, l1b, wg, wu, wd, l2g, l2b, alpha, tm):
    m, d = x2d.shape
    dff = wg.shape[1]
    tm = min(tm, m)
    assert m % tm == 0
    row = lambda i: (i, 0)
    return pl.pallas_call(
        functools.partial(_post_kernel, alpha=alpha),
        grid=(m // tm,),
        in_specs=[
            pl.BlockSpec((tm, d), row),
            pl.BlockSpec((tm, GV_W), row),
            pl.BlockSpec((tm, DSA_W), row),
            _const_spec((GV_W + DSA_W, d)),
            _const_spec((1, d)),
            _const_spec((1, d)),
            _const_spec((d, dff)),
            _const_spec((d, dff)),
            _const_spec((dff, d)),
            _const_spec((1, d)),
            _const_spec((1, d)),
        ],
        out_specs=pl.BlockSpec((tm, d), row),
        out_shape=jax.ShapeDtypeStruct((m, d), F32),
        compiler_params=_cparams(("parallel",)),
        name="post",
    )(x2d, gated, dsa, wo, l1g, l1b, wg, wu, wd, l2g, l2b)


def _layer_weights(l, w_in, w_lr_up, b_gate, gla_norm_g, w_o, ln1_g, ln1_b,
                   w_ffn_gate, w_ffn_up, w_ffn_down, ln2_g, ln2_b):
    row = lambda a: a[l][None, :].astype(F32)
    w, wvt = _prep_w_in(w_in[l])
    return dict(
        w=w, wvt=wvt, wlr=_prep_w_lr(w_lr_up[l]), bg=row(b_gate), gn=row(gla_norm_g),
        wo=w_o[l].astype(BF16), l1g=row(ln1_g), l1b=row(ln1_b),
        wg=w_ffn_gate[l].astype(BF16), wu=w_ffn_up[l].astype(BF16), wd=w_ffn_down[l].astype(BF16),
        l2g=row(ln2_g), l2b=row(ln2_b))


def _prompt_layer(x, wts, alpha):
    bsz, length, d = x.shape
    tm = 512 if length % 512 == 0 else length
    tabs = _rope_tables(jnp.arange(length, dtype=F32))
    x2d = x.reshape(bsz * length, d)
    (gq, gk, gv, gr, gg, dq, dkb, iq, ikb, kt, vt, vtb, ikt, iwt) = _project_fm(
        x2d, wts["w"], wts["wvt"], wts["wlr"], wts["bg"], *tabs, tm, bsz)
    r3 = lambda a: a.reshape(bsz, length, a.shape[-1])
    chunk = int(np.gcd(length, GLA_CHUNK))
    n_sub = 4 if length % (4 * chunk) == 0 else 1
    s0 = jnp.zeros((bsz, GLA_HEADS, GLA_DK, GLA_DV), F32)
    gated, state = _gla(r3(gq), r3(gk), r3(gg), r3(gv), r3(gr), s0, wts["gn"], chunk, n_sub,
                        4 if bsz % 4 == 0 else 1)
    kt_tile = 256 if length % 256 == 0 else 128
    dsa = _dsa_prompt(r3(dq), r3(iq), iwt, r3(dkb), vtb, r3(ikb), 128, kt_tile)
    y = _post(x2d, gated.reshape(bsz * length, GV_W), dsa.reshape(bsz * length, DSA_W),
              wts["wo"], wts["l1g"], wts["l1b"], wts["wg"], wts["wu"], wts["wd"],
              wts["l2g"], wts["l2b"], alpha, 256)
    heads = lambda a: jnp.transpose(a.reshape(bsz, DSA_HEADS, DSA_HEAD_DIM, length), (0, 3, 1, 2))
    return (y.reshape(bsz, length, d), heads(kt), heads(vt), jnp.transpose(ikt, (0, 2, 1)), state)


def _sample_layer(x, wts, alpha, cache_k, cache_v, cache_kidx, state, page_table):
    bsz, n_t, d = x.shape
    n_pages = page_table.shape[1]
    past = n_pages * PAGE_SIZE
    m = bsz * n_t
    pos = past + jnp.arange(n_t, dtype=F32)
    tabs = [jnp.tile(t, (bsz, 1)) for t in _rope_tables(pos)]
    x2d = x.reshape(m, d)
    (gq, gk, gv, gr, gg, dq, dkb, iq, ikb, dk, dv, dvb, ik, iw) = _project(
        x2d, wts["w"], wts["wlr"], wts["bg"], *tabs, m)
    chunk = 16
    n_seq = 8 if bsz % 8 == 0 else 1
    pad3 = lambda a: jnp.pad(a.reshape(bsz, n_t, a.shape[-1]), ((0, 0), (0, chunk - n_t), (0, 0)))
    gated, state_new = _gla(pad3(gq), pad3(gk), pad3(gg), pad3(gv), pad3(gr), state, wts["gn"],
                            chunk, 1, n_seq)
    gated = gated[:, :n_t].reshape(m, GV_W)
    topk = min(IDX_TOPK, (past + n_t) // 4)
    pick = lambda want: max(c for c in (1, 2, 4, 8, 16, 32, 64) if c <= want and n_pages % c == 0)
    pt_flat = page_table.reshape(-1)
    q_rows = iq.reshape(bsz, n_t * IDX_HEADS, IDX_DIM)
    w_rows = iw.reshape(bsz, n_t * IDX_HEADS, 1)
    ik_new = jnp.pad(ikb[:, :IDX_DIM].reshape(bsz, n_t, IDX_DIM), ((0, 0), (0, LANES - n_t), (0, 0)))
    score, snew = _idx_sample(pt_flat, q_rows, w_rows, ik_new, cache_kidx, n_t, n_pages, pick(64))
    thr, pcut = _thr_sample(score.reshape(m, past), snew.reshape(m, LANES), n_t, past, topk,
                            64 if m % 64 == 0 else 32)
    n_new = 16
    new3 = lambda a: jnp.pad(a.reshape(bsz, n_t, DSA_W), ((0, 0), (0, n_new - n_t), (0, 0)))
    dsa = _attn_sample(pt_flat, dq.astype(F32).reshape(bsz, n_t, DSA_W), score, snew,
                       thr.reshape(bsz, n_t, 1), pcut.reshape(bsz, n_t, 1), new3(dkb), new3(dvb),
                       cache_k, cache_v, n_t, n_pages, pick(16), past, topk)
    y = _post(x2d, gated, dsa.reshape(m, DSA_W).astype(BF16),
              wts["wo"], wts["l1g"], wts["l1b"], wts["wg"], wts["wu"], wts["wd"],
              wts["l2g"], wts["l2b"], alpha, 256)
    return (y.reshape(bsz, n_t, d),
            dk.reshape(bsz, n_t, DSA_HEADS, DSA_HEAD_DIM),
            dv.reshape(bsz, n_t, DSA_HEADS, DSA_HEAD_DIM),
            ik.reshape(bsz, n_t, IDX_DIM), state_new)


def kernel(x_prompt, x_sample, cache_k, cache_v, cache_kidx, state_gla, page_table, w_in, w_lr_up, b_gate, gla_norm_g, w_o, ln1_g, ln1_b, w_ffn_gate, w_ffn_up, w_ffn_down, ln2_g, ln2_b):
    depth = w_in.shape[0]
    alpha = float((2 * depth) ** 0.25)
    n_pool = cache_k.shape[1]
    xp, xs = x_prompt, x_sample
    outs_p = [[] for _ in range(4)]
    outs_s = [[] for _ in range(4)]
    for l in range(depth):
        wts = _layer_weights(l, w_in, w_lr_up, b_gate, gla_norm_g, w_o, ln1_g, ln1_b,
                             w_ffn_gate, w_ffn_up, w_ffn_down, ln2_g, ln2_b)
        xp, *rest_p = _prompt_layer(xp, wts, alpha)
        feat = lambda c: jnp.transpose(c, (0, 1, 3, 4, 2)).reshape(
            depth * n_pool, DSA_HEADS, DSA_HEAD_DIM, PAGE_SIZE)
        xs, *rest_s = _sample_layer(
            xs, wts, alpha, feat(cache_k), feat(cache_v),
            jnp.transpose(cache_kidx, (0, 1, 3, 2)).reshape(depth * n_pool, IDX_DIM, PAGE_SIZE),
            state_gla[l], page_table + l * n_pool)
        for acc, val in zip(outs_p, rest_p):
            acc.append(val)
        for acc, val in zip(outs_s, rest_s):
            acc.append(val)
    stack = lambda seq: jnp.stack(seq)
    return (xp, xs, *[stack(a) for a in outs_p], *[stack(a) for a in outs_s])
```

```python
import functools

import numpy as np
import jax
import jax.numpy as jnp
from jax import lax
from jax.experimental import pallas as pl
from jax.experimental.pallas import tpu as pltpu

F32 = jnp.float32
BF16 = jnp.bfloat16
I32 = jnp.int32

GLA_HEADS = 4
GLA_DK = 64
GLA_DV = 128
GLA_LR_RANK = 16
GATE_TAU = 16.0
GLA_CHUNK = 64
EXP_CLAMP = 80.0
DSA_HEADS = 8
DSA_HEAD_DIM = 64
ROT_DIM = 16
IDX_HEADS = 8
IDX_DIM = 64
IDX_TOPK = 256
PAGE_SIZE = 128
ROPE_THETA = 500000.0
NORM_EPS = 1e-5

LANES = 128
SUBLANES = 8
ROW_CHUNK = 64
INT_MIN = int(np.iinfo(np.int32).min)
INT_MAX = int(np.iinfo(np.int32).max)
VMEM_LIMIT = 56 * 1024 * 1024

GQ_W = GLA_HEADS * GLA_DK
GV_W = GLA_HEADS * GLA_DV
DSA_W = DSA_HEADS * DSA_HEAD_DIM
IDX_W = IDX_HEADS * IDX_DIM
OFF_GQ = 0
OFF_GK = OFF_GQ + GQ_W
OFF_GV = OFF_GK + GQ_W
OFF_GR = OFF_GV + GV_W
OFF_DQ = OFF_GR + GV_W
OFF_DK = OFF_DQ + DSA_W
OFF_DV = OFF_DK + DSA_W
OFF_IQ = OFF_DV + DSA_W
OFF_MISC = OFF_IQ + IDX_W
N_COLS = OFF_MISC + LANES
MISC_GLR = IDX_DIM
MISC_IW = IDX_DIM + GLA_LR_RANK

NT_DIMS = (((1,), (1,)), ((), ()))
TN_DIMS = (((0,), (0,)), ((), ()))


def _dot(a, b):
    return jnp.dot(a, b, preferred_element_type=F32)


def _dot_nt(a, b):
    return lax.dot_general(a, b, NT_DIMS, preferred_element_type=F32)


def _dot_tn(a, b):
    return lax.dot_general(a, b, TN_DIMS, preferred_element_type=F32)


def _cparams(sem):
    return pltpu.CompilerParams(dimension_semantics=sem, vmem_limit_bytes=VMEM_LIMIT)


def _const_spec(shape):
    nd = len(shape)
    return pl.BlockSpec(shape, lambda *_: (0,) * nd, pipeline_mode=pl.Buffered(1))


def _proj_shared(x_ref, w_ref, wlr_ref, bg_ref, cos_ref, sina_ref, sinb_ref,
                 gq_ref, gk_ref, gv_ref, gr_ref, gg_ref, dq_ref, dkb_ref, iq_ref, ikb_ref):
    xb = x_ref[...].astype(BF16)
    cosp = cos_ref[...]
    sina = sina_ref[...]
    sinb = sinb_ref[...]

    def seg(off, width):
        return _dot(xb, w_ref[:, off:off + width])

    def rope(u):
        return u * cosp + pltpu.roll(u, LANES - ROT_DIM // 2, 1) * sina + pltpu.roll(u, ROT_DIM // 2, 1) * sinb

    gq_ref[...] = seg(OFF_GQ, GQ_W) * (GLA_DK ** -0.5)
    gk_ref[...] = seg(OFF_GK, GQ_W)
    gv_ref[...] = seg(OFF_GV, GV_W).astype(BF16)
    gr_ref[...] = seg(OFF_GR, GV_W).astype(BF16)
    uq = seg(OFF_DQ, DSA_W)
    uk = seg(OFF_DK, DSA_W)
    ui = seg(OFF_IQ, IDX_W)
    keys = []
    for i in range(DSA_W // LANES):
        sl = slice(i * LANES, (i + 1) * LANES)
        dq_ref[:, sl] = (rope(uq[:, sl]) * (DSA_HEAD_DIM ** -0.5)).astype(BF16)
        k = rope(uk[:, sl])
        dkb_ref[:, sl] = k.astype(BF16)
        keys.append(k)
        iq_ref[:, sl] = (rope(ui[:, sl]) * (IDX_DIM ** -0.5)).astype(BF16)
    misc = seg(OFF_MISC, LANES)
    mr = rope(misc)
    lane = lax.broadcasted_iota(I32, (1, LANES), 1)
    ikb_ref[...] = jnp.where(lane < IDX_DIM, mr, pltpu.roll(mr, IDX_DIM, 1)).astype(BF16)
    z = _dot(misc.astype(BF16), wlr_ref[...]) + bg_ref[...]
    gg_ref[...] = (jnp.minimum(z, 0.0) - jnp.log1p(jnp.exp(-jnp.abs(z)))) * (1.0 / GATE_TAU)
    return xb, keys, misc, mr


def _proj_kernel(x_ref, w_ref, wlr_ref, bg_ref, cos_ref, sina_ref, sinb_ref,
                 gq_ref, gk_ref, gv_ref, gr_ref, gg_ref, dq_ref, dkb_ref, iq_ref, ikb_ref,
                 dk_ref, dv_ref, dvb_ref, ik_ref, iw_ref):
    xb, keys, misc, mr = _proj_shared(x_ref, w_ref, wlr_ref, bg_ref, cos_ref, sina_ref, sinb_ref,
                                      gq_ref, gk_ref, gv_ref, gr_ref, gg_ref, dq_ref, dkb_ref, iq_ref, ikb_ref)
    for i, k in enumerate(keys):
        dk_ref[:, i * LANES:(i + 1) * LANES] = k
    v = _dot(xb, w_ref[:, OFF_DV:OFF_DV + DSA_W])
    dv_ref[...] = v
    dvb_ref[...] = v.astype(BF16)
    ik_ref[...] = mr[:, :IDX_DIM]
    iw_ref[...] = misc[:, MISC_IW:MISC_IW + IDX_HEADS] * (IDX_HEADS ** -0.5)


def _proj_fm_kernel(x_ref, w_ref, wvt_ref, wlr_ref, bg_ref, cos_ref, sina_ref, sinb_ref,
                    gq_ref, gk_ref, gv_ref, gr_ref, gg_ref, dq_ref, dkb_ref, iq_ref, ikb_ref,
                    kt_ref, vt_ref, vtb_ref, ikt_ref, iwt_ref):
    xb, keys, misc, mr = _proj_shared(x_ref, w_ref, wlr_ref, bg_ref, cos_ref, sina_ref, sinb_ref,
                                      gq_ref, gk_ref, gv_ref, gr_ref, gg_ref, dq_ref, dkb_ref, iq_ref, ikb_ref)
    for i, k in enumerate(keys):
        kt_ref[0, i * LANES:(i + 1) * LANES, :] = k.T
    vt = _dot_nt(wvt_ref[...], xb)
    vt_ref[0] = vt
    vtb_ref[0] = vt.astype(BF16)
    mt = mr.T
    ikt_ref[0] = mt[:IDX_DIM]
    iwt_ref[0] = mt[MISC_IW:MISC_IW + IDX_HEADS] * (IDX_HEADS ** -0.5)


_PROJ_SHARED_OUTS = [
    (GQ_W, F32), (GQ_W, F32), (GV_W, BF16), (GV_W, BF16), (GQ_W, F32),
    (DSA_W, BF16), (DSA_W, BF16), (IDX_W, BF16), (LANES, BF16),
]


def _project(x2d, w, wlr, bg, cos_t, sina_t, sinb_t, tm):
    m, d = x2d.shape
    period = cos_t.shape[0]
    assert m % tm == 0 and period % tm == 0
    nper = period // tm
    row = lambda i: (i, 0)
    tab = lambda i: (i % nper, 0)
    outs = _PROJ_SHARED_OUTS + [(DSA_W, F32), (DSA_W, F32), (DSA_W, BF16), (IDX_DIM, F32), (IDX_HEADS, F32)]
    return pl.pallas_call(
        _proj_kernel,
        grid=(m // tm,),
        in_specs=[
            pl.BlockSpec((tm, d), row),
            _const_spec((d, N_COLS)),
            _const_spec((LANES, GQ_W)),
            _const_spec((1, GQ_W)),
            pl.BlockSpec((tm, LANES), tab),
            pl.BlockSpec((tm, LANES), tab),
            pl.BlockSpec((tm, LANES), tab),
        ],
        out_specs=[pl.BlockSpec((tm, wd), row) for wd, _ in outs],
        out_shape=[jax.ShapeDtypeStruct((m, wd), dt) for wd, dt in outs],
        compiler_params=_cparams(("parallel",)),
        name="project",
    )(x2d, w, wlr, bg, cos_t, sina_t, sinb_t)


def _project_fm(x2d, w, wvt, wlr, bg, cos_t, sina_t, sinb_t, tm, bsz):
    m, d = x2d.shape
    length = m // bsz
    assert length % tm == 0 and cos_t.shape[0] == length
    nper = length // tm
    row = lambda i: (i, 0)
    tab = lambda i: (i % nper, 0)
    fm = lambda i: (i // nper, 0, i % nper)
    fm_outs = [(DSA_W, F32), (DSA_W, F32), (DSA_W, BF16), (IDX_DIM, F32), (IDX_HEADS, F32)]
    return pl.pallas_call(
        _proj_fm_kernel,
        grid=(m // tm,),
        in_specs=[
            pl.BlockSpec((tm, d), row),
            _const_spec((d, N_COLS)),
            _const_spec((DSA_W, d)),
            _const_spec((LANES, GQ_W)),
            _const_spec((1, GQ_W)),
            pl.BlockSpec((tm, LANES), tab),
            pl.BlockSpec((tm, LANES), tab),
            pl.BlockSpec((tm, LANES), tab),
        ],
        out_specs=[pl.BlockSpec((tm, wd), row) for wd, _ in _PROJ_SHARED_OUTS]
        + [pl.BlockSpec((1, wd, tm), fm) for wd, _ in fm_outs],
        out_shape=[jax.ShapeDtypeStruct((m, wd), dt) for wd, dt in _PROJ_SHARED_OUTS]
        + [jax.ShapeDtypeStruct((bsz, wd, length), dt) for wd, dt in fm_outs],
        compiler_params=_cparams(("parallel",)),
        name="project_fm",
    )(x2d, w, wvt, wlr, bg, cos_t, sina_t, sinb_t)


_IN_SIZES = (GQ_W, GQ_W, GV_W, GV_W, GLA_LR_RANK, DSA_W, DSA_W, DSA_W, IDX_W, IDX_DIM, IDX_HEADS)


def _prep_w_in(w_in):
    cuts = np.cumsum(_IN_SIZES)[:-1].tolist()
    gq, gk, gv, gr, glr, dq, dk, dv, iq, ik, iw = jnp.split(w_in, cuts, axis=1)
    pad = jnp.zeros((w_in.shape[0], LANES - IDX_DIM - GLA_LR_RANK - IDX_HEADS), w_in.dtype)
    w = jnp.concatenate([gq, gk, gv, gr, dq, dk, dv, iq, ik, glr, iw, pad], axis=1).astype(BF16)
    return w, jnp.transpose(dv).astype(BF16)


def _prep_w_lr(w_lr_up):
    z0 = jnp.zeros((MISC_GLR, GQ_W), w_lr_up.dtype)
    z1 = jnp.zeros((LANES - MISC_GLR - GLA_LR_RANK, GQ_W), w_lr_up.dtype)
    return jnp.concatenate([z0, w_lr_up, z1], axis=0).astype(BF16)


def _rope_tables(pos):
    half = ROT_DIM // 2
    inv = jnp.power(ROPE_THETA, -jnp.arange(half, dtype=F32) * (2.0 / ROT_DIM))
    ang = pos[:, None] * inv[None, :]
    cos = jnp.cos(ang)
    sin = jnp.sin(ang)
    n = pos.shape[0]
    rest = DSA_HEAD_DIM - ROT_DIM
    z8 = jnp.zeros((n, half), F32)
    zr = jnp.zeros((n, rest), F32)
    cos_h = jnp.concatenate([cos, cos, jnp.ones((n, rest), F32)], axis=1)
    sina_h = jnp.concatenate([-sin, z8, zr], axis=1)
    sinb_h = jnp.concatenate([z8, sin, zr], axis=1)
    rep = LANES // DSA_HEAD_DIM
    return jnp.tile(cos_h, (1, rep)), jnp.tile(sina_h, (1, rep)), jnp.tile(sinb_h, (1, rep))


def _gla_kernel(gq_ref, gk_ref, gg_ref, gv_ref, gr_ref, s0_ref, gn_ref, o_ref, sout_ref, st_scr,
                *, chunk, n_sub, n_seq):
    j = pl.program_id(1)
    lane = lax.broadcasted_iota(I32, (1, LANES), 1)
    lo = lane < GLA_DK

    @pl.when(j == 0)
    def _():
        zero = jnp.zeros((GLA_DK, GLA_DV), F32)
        for s in range(n_seq):
            for h in range(GLA_HEADS):
                s0 = s0_ref[s, h]
                full = jnp.concatenate([s0, zero] if h % 2 == 0 else [zero, s0], axis=0)
                st_scr[s * GLA_HEADS + h] = full.T

    r = lax.broadcasted_iota(I32, (chunk, chunk), 0)
    c = lax.broadcasted_iota(I32, (chunk, chunk), 1)
    tri = r >= c
    trib = jnp.where(tri, 1.0, 0.0).astype(BF16)
    gn = gn_ref[...]
    parts = []
    for s in range(n_seq):
        for ci in range(n_sub):
            rows = slice(ci * chunk, (ci + 1) * chunk)
            g = gg_ref[s, rows, :]
            g_hi = g.astype(BF16)
            g_lo = (g - g_hi.astype(F32)).astype(BF16)
            b = _dot(trib, g_hi) + _dot(trib, g_lo)
            b_last = b[chunk - 1:chunk, :]
            gq = gq_ref[s, rows, :]
            q = gq * jnp.exp(b)
            k = gk_ref[s, rows, :]
            k_out = k * jnp.exp(b_last - b)
            d_last = jnp.exp(b_last)
            ref = b[chunk // 2 - 1:chunk // 2, :]
            q_in = gq * jnp.exp(jnp.minimum(b - ref, EXP_CLAMP))
            k_in = k * jnp.exp(jnp.minimum(ref - b, EXP_CLAMP))
            for h in range(GLA_HEADS):
                p, hh = divmod(h, 2)
                sl = slice(p * LANES, (p + 1) * LANES)
                msk = lo if hh == 0 else jnp.logical_not(lo)
                vh = gv_ref[s, rows, h * GLA_DV:(h + 1) * GLA_DV]
                a = _dot_nt(jnp.where(msk, q_in[:, sl], 0.0).astype(BF16), k_in[:, sl].astype(BF16))
                a = jnp.where(tri, a, 0.0).astype(BF16)
                qm = jnp.where(msk, q[:, sl], 0.0).astype(BF16)
                km = jnp.where(msk, k_out[:, sl], 0.0).astype(BF16)
                parts.append((qm, _dot(a, vh), _dot_tn(vh, km), d_last[:, sl]))
    it = iter(parts)
    for s in range(n_seq):
        for ci in range(n_sub):
            rows = slice(ci * chunk, (ci + 1) * chunk)
            for h in range(GLA_HEADS):
                qm, o_intra, kv, decay = next(it)
                vsl = slice(h * GLA_DV, (h + 1) * GLA_DV)
                st = st_scr[s * GLA_HEADS + h]
                o = o_intra + _dot_nt(qm, st.astype(BF16))
                st_scr[s * GLA_HEADS + h] = st * decay + kv
                ms = jnp.mean(o * o, axis=1, keepdims=True)
                of = o * lax.rsqrt(ms + NORM_EPS) * gn
                gr = gr_ref[s, rows, vsl].astype(F32)
                o_ref[s, rows, vsl] = (of * (gr * jax.nn.sigmoid(gr))).astype(BF16)

    @pl.when(j == pl.num_programs(1) - 1)
    def _():
        for s in range(n_seq):
            for h in range(GLA_HEADS):
                off = (h % 2) * GLA_DK
                sout_ref[s, h] = st_scr[s * GLA_HEADS + h].T[off:off + GLA_DK, :]


def _gla(gq, gk, gg, gv, gr, s0, gn, chunk, n_sub, n_seq):
    bsz, length, _ = gq.shape
    tl = chunk * n_sub
    assert length % tl == 0 and bsz % n_seq == 0
    tok = lambda b, j: (b, j, 0)
    st = lambda b, j: (b, 0, 0, 0)
    return pl.pallas_call(
        functools.partial(_gla_kernel, chunk=chunk, n_sub=n_sub, n_seq=n_seq),
        grid=(bsz // n_seq, length // tl),
        in_specs=[
            pl.BlockSpec((n_seq, tl, GQ_W), tok),
            pl.BlockSpec((n_seq, tl, GQ_W), tok),
            pl.BlockSpec((n_seq, tl, GQ_W), tok),
            pl.BlockSpec((n_seq, tl, GV_W), tok),
            pl.BlockSpec((n_seq, tl, GV_W), tok),
            pl.BlockSpec((n_seq, GLA_HEADS, GLA_DK, GLA_DV), st),
            pl.BlockSpec((1, GLA_DV), lambda b, j: (0, 0)),
        ],
        out_specs=[
            pl.BlockSpec((n_seq, tl, GV_W), tok),
            pl.BlockSpec((n_seq, GLA_HEADS, GLA_DK, GLA_DV), st),
        ],
        out_shape=[
            jax.ShapeDtypeStruct((bsz, length, GV_W), BF16),
            jax.ShapeDtypeStruct((bsz, GLA_HEADS, GLA_DK, GLA_DV), F32),
        ],
        scratch_shapes=[pltpu.VMEM((n_seq * GLA_HEADS, GLA_DV, LANES), F32)],
        compiler_params=_cparams(("parallel", "arbitrary")),
        name="gla",
    )(gq, gk, gg, gv, gr, s0, gn)


def _mask_scores(score, causal):
    return jnp.where(causal, jnp.where(score == 0.0, 0.0, score), -jnp.inf)


def _ordered_float(key):
    return pltpu.bitcast(key ^ ((key >> 31) & INT_MAX), F32)


def _topk_threshold(count, shape, topk, pcut_ref, idx_bits):
    kf = float(topk)
    cnt0 = count(lambda sc, kpos: sc >= 0.0)
    key0 = jnp.where(cnt0 >= kf, 0, INT_MIN).astype(I32)

    def body(i, key):
        cand = key + jnp.left_shift(jnp.int32(1), 30 - i)
        cand_f = _ordered_float(cand)
        cnt = count(lambda sc, kpos: sc >= cand_f)
        return jnp.where(cnt >= kf, cand, key)

    thr = _ordered_float(lax.fori_loop(0, 31, body, key0))
    need = kf - count(lambda sc, kpos: sc > thr)
    n_eq = count(lambda sc, kpos: sc == thr)
    cut = n_eq > need
    pcut_ref[...] = jnp.full(shape, INT_MAX, I32)

    @pl.when(jnp.max(jnp.where(cut, 1.0, 0.0)) > 0.0)
    def _():
        def pbody(i, pos):
            cand = pos + jnp.left_shift(jnp.int32(1), idx_bits - 1 - i)
            cnt = count(lambda sc, kpos: jnp.logical_and(sc == thr, kpos < cand))
            return jnp.where(cnt < need, cand, pos)

        pos = lax.fori_loop(0, idx_bits, pbody, jnp.zeros(shape, I32))
        pcut_ref[...] = jnp.where(cut, pos, INT_MAX)

    return thr, pcut_ref[...]


def _select(score, thr, pcut, kpos, qpos, topk):
    keep = jnp.logical_or(score > thr, jnp.logical_and(score == thr, kpos <= pcut))
    keep = jnp.logical_or(keep, qpos < topk)
    return jnp.logical_and(keep, kpos <= qpos)


def _dsa_kernel(dq_ref, iq_ref, iwt_ref, k_ref, vt_ref, ik_ref, o_ref,
                key_scr, bias_scr, logit_scr, p_scr, out_scr, pcut_scr, *, tq, kt, nkt, qb0, topk, idx_bits):
    j = pl.program_id(1)
    t0 = (qb0 + j) * tq
    lane = lax.broadcasted_iota(I32, (1, LANES), 1)
    lo = lane < DSA_HEAD_DIM
    qpos = t0 + lax.broadcasted_iota(I32, (1, tq), 1)
    wt = iwt_ref[0]

    def head_pair(ref, p):
        qs = ref[0, :, p * LANES:(p + 1) * LANES]
        zero = jnp.zeros_like(qs)
        return jnp.concatenate([jnp.where(lo, qs, zero), jnp.where(lo, zero, qs)], axis=0)

    def rows_of(c):
        return pl.ds(pl.multiple_of(c * kt, kt), kt)

    def kpos_of(c):
        return c * kt + lax.broadcasted_iota(I32, (kt, tq), 0)

    def col_sum(x):
        return jnp.sum(x.reshape(kt // SUBLANES, SUBLANES, tq), axis=0)

    n_pair = IDX_HEADS // 2
    iq2 = [head_pair(iq_ref, p) for p in range(n_pair)]
    w2 = [jnp.concatenate([wt[2 * p:2 * p + 1, :], wt[2 * p + 1:2 * p + 2, :]], axis=1) for p in range(n_pair)]

    def score_tile(c, carry):
        rs = pl.ds(pl.multiple_of(c * tq, tq), tq)
        ik_t = ik_ref[0, rs, :]
        acc2 = jnp.zeros((tq, 2 * tq), F32)
        for p in range(n_pair):
            acc2 = acc2 + jnp.maximum(_dot_nt(ik_t, iq2[p]), 0.0) * w2[p]
        kpos = c * tq + lax.broadcasted_iota(I32, (tq, tq), 0)
        key_scr[rs, :] = _mask_scores(acc2[:, :tq] + acc2[:, tq:], kpos <= qpos)
        return carry

    lax.fori_loop(0, nkt * kt // tq, score_tile, 0, unroll=4 if (nkt * kt // tq) % 4 == 0 else 2)

    def count(pred):
        acc = jnp.zeros((SUBLANES, tq), F32)
        for c in range(nkt):
            hit = pred(key_scr[c * kt:(c + 1) * kt, :], kpos_of(c))
            acc = acc + col_sum(jnp.where(hit, 1.0, 0.0))
        return jnp.sum(acc, axis=0, keepdims=True)

    thr, pcut = _topk_threshold(count, (1, tq), topk, pcut_scr, idx_bits)

    def bias_tile(c, carry):
        sel = _select(key_scr[rows_of(c), :], thr, pcut, kpos_of(c), qpos, topk)
        bias_scr[rows_of(c), :] = jnp.where(sel, 0.0, -jnp.inf)
        return carry

    lax.fori_loop(0, nkt, bias_tile, 0)

    extent = nkt * kt
    n_chunk = extent // ROW_CHUNK
    n_pair = DSA_HEADS // 2
    q2s = [head_pair(dq_ref, p) for p in range(n_pair)]

    def chunk_rows(i):
        return pl.ds(pl.multiple_of(i * ROW_CHUNK, ROW_CHUNK), ROW_CHUNK)

    def col_max(x):
        return jnp.max(x.reshape(kt // SUBLANES, SUBLANES, tq), axis=0)

    def logit_body(c, carry):
        rs = rows_of(c)
        b = bias_scr[rs, :]
        out = []
        for p in range(n_pair):
            x = _dot_nt(k_ref[0, rs, p * LANES:(p + 1) * LANES], q2s[p])
            xa = x[:, :tq] + b
            xb = x[:, tq:] + b
            logit_scr[rs, 2 * p * tq:(2 * p + 2) * tq] = jnp.concatenate([xa, xb], axis=1)
            out.append(jnp.maximum(carry[2 * p], col_max(xa)))
            out.append(jnp.maximum(carry[2 * p + 1], col_max(xb)))
        return tuple(out)

    neg = jnp.full((SUBLANES, tq), -jnp.inf, F32)
    m8 = lax.fori_loop(0, nkt, logit_body, (neg,) * DSA_HEADS, unroll=2 if nkt % 2 == 0 else 1)
    m_all = jnp.concatenate([jnp.max(m, axis=0, keepdims=True) for m in m8], axis=1)

    def exp_body(i, l8):
        e = jnp.exp(logit_scr[chunk_rows(i), :] - m_all)
        p_scr[chunk_rows(i), :] = e.astype(BF16)
        return l8 + jnp.sum(e.reshape(ROW_CHUNK // SUBLANES, SUBLANES, DSA_HEADS * tq), axis=0)

    l8 = lax.fori_loop(0, n_chunk, exp_body, jnp.zeros((SUBLANES, DSA_HEADS * tq), F32), unroll=2)
    l_all = jnp.sum(l8, axis=0, keepdims=True)
    for h in range(DSA_HEADS):
        vsl = slice(h * DSA_HEAD_DIM, (h + 1) * DSA_HEAD_DIM)
        qsl = slice(h * tq, (h + 1) * tq)
        out_scr[vsl, :] = _dot(vt_ref[0, vsl, :], p_scr[:, qsl]) / l_all[:, qsl]

    o_ref[0] = out_scr[...].T.astype(BF16)


def _dsa_prompt_bucket(dq, iq, iwt, dkb, vtb, ikb, tq, kt, qb0, nqb, topk, idx_bits):
    bsz = dq.shape[0]
    extent = (qb0 + nqb) * tq
    assert extent % kt == 0
    qblk = lambda b, j: (b, qb0 + j, 0)
    head = lambda b, j: (b, 0, 0)
    return pl.pallas_call(
        functools.partial(_dsa_kernel, tq=tq, kt=kt, nkt=extent // kt, qb0=qb0, topk=topk, idx_bits=idx_bits),
        grid=(bsz, nqb),
        in_specs=[
            pl.BlockSpec((1, tq, DSA_W), qblk),
            pl.BlockSpec((1, tq, IDX_W), qblk),
            pl.BlockSpec((1, IDX_HEADS, tq), lambda b, j: (b, 0, qb0 + j)),
            pl.BlockSpec((1, extent, DSA_W), head),
            pl.BlockSpec((1, DSA_W, extent), head),
            pl.BlockSpec((1, extent, LANES), head),
        ],
        out_specs=pl.BlockSpec((1, tq, DSA_W), lambda b, j: (b, j, 0)),
        out_shape=jax.ShapeDtypeStruct((bsz, nqb * tq, DSA_W), BF16),
        scratch_shapes=[
            pltpu.VMEM((extent, tq), F32),
            pltpu.VMEM((extent, tq), F32),
            pltpu.VMEM((extent, DSA_HEADS * tq), F32),
            pltpu.VMEM((extent, DSA_HEADS * tq), BF16),
            pltpu.VMEM((DSA_W, tq), F32),
            pltpu.VMEM((1, tq), I32),
        ],
        compiler_params=_cparams(("parallel", "arbitrary")),
        name="dsa_prompt",
    )(dq, iq, iwt, dkb, vtb, ikb)


def _dsa_prompt(dq, iq, iwt, dkb, vtb, ikb, tq, kt):
    length = dq.shape[1]
    topk = min(IDX_TOPK, length // 4)
    nqb = kt // tq
    assert length % kt == 0 and kt % tq == 0
    idx_bits = int(length - 1).bit_length()
    outs = [_dsa_prompt_bucket(dq, iq, iwt, dkb, vtb, ikb, tq, kt, qb0, nqb, topk, idx_bits)
            for qb0 in range(0, length // tq, nqb)]
    return jnp.concatenate(outs, axis=1)


def _fetch_pages(pt_ref, hbm_refs, bufs, sems, n_pg):
    ngrp = pl.num_programs(1)
    s = pl.program_id(0) * ngrp + pl.program_id(1)
    total = pl.num_programs(0) * ngrp
    slot = s % 2

    def copies(step, slot_):
        return [pltpu.make_async_copy(hbm.at[pt_ref[step * n_pg + i]], buf.at[slot_, i], sems.at[slot_, a])
                for a, (hbm, buf) in enumerate(zip(hbm_refs, bufs)) for i in range(n_pg)]

    def start_all(cs):
        for n, c in enumerate(cs):
            c.start(priority=n % 2)

    @pl.when(s == 0)
    def _():
        start_all(copies(0, 0))

    @pl.when(s + 1 < total)
    def _():
        start_all(copies(s + 1, 1 - slot))

    for c in copies(s, slot):
        c.wait()
    return slot


def _idx_sample_kernel(pt_ref, q_ref, w_ref, iknew_ref, kidx_hbm, score_ref, snew_ref, page_buf, sems, *, n_pg):
    slot = _fetch_pages(pt_ref, [kidx_hbm], [page_buf], sems, n_pg)
    q = q_ref[0]
    w = w_ref[0]
    n_t = score_ref.shape[1]

    def head_sum(s):
        r = jnp.maximum(s, 0.0) * w
        return jnp.sum(r.reshape(n_t, IDX_HEADS, r.shape[1]), axis=1)

    for i in range(n_pg):
        score_ref[0, :, i * PAGE_SIZE:(i + 1) * PAGE_SIZE] = head_sum(_dot(q, page_buf[slot, i].astype(BF16)))

    @pl.when(pl.program_id(1) == 0)
    def _():
        snew_ref[0] = head_sum(_dot_nt(q, iknew_ref[0]))


def _idx_sample(page_table_flat, q_rows, w_rows, ik_new, kidx_pages, n_t, n_pages, n_pg):
    bsz = q_rows.shape[0]
    ngrp = n_pages // n_pg
    per_b = lambda b, g, pt: (b, 0, 0)
    grid_spec = pltpu.PrefetchScalarGridSpec(
        num_scalar_prefetch=1,
        grid=(bsz, ngrp),
        in_specs=[
            pl.BlockSpec((1, n_t * IDX_HEADS, IDX_DIM), per_b),
            pl.BlockSpec((1, n_t * IDX_HEADS, 1), per_b),
            pl.BlockSpec((1, LANES, IDX_DIM), per_b),
            pl.BlockSpec(memory_space=pl.ANY),
        ],
        out_specs=[
            pl.BlockSpec((1, n_t, n_pg * PAGE_SIZE), lambda b, g, pt: (b, 0, g)),
            pl.BlockSpec((1, n_t, LANES), per_b),
        ],
        scratch_shapes=[
            pltpu.VMEM((2, n_pg, IDX_DIM, PAGE_SIZE), F32),
            pltpu.SemaphoreType.DMA((2, 1)),
        ],
    )
    return pl.pallas_call(
        functools.partial(_idx_sample_kernel, n_pg=n_pg),
        grid_spec=grid_spec,
        out_shape=[
            jax.ShapeDtypeStruct((bsz, n_t, n_pages * PAGE_SIZE), F32),
            jax.ShapeDtypeStruct((bsz, n_t, LANES), F32),
        ],
        compiler_params=_cparams(("arbitrary", "arbitrary")),
        name="idx_sample",
    )(page_table_flat, q_rows, w_rows, ik_new, kidx_pages)


def _thr_sample_kernel(score_ref, snew_ref, thr_ref, pcut_ref, key_scr, pcut_scr, *, n_t, past, topk):
    rows, width = key_scr.shape
    row0 = pl.program_id(0) * rows
    kpos = lax.broadcasted_iota(I32, (rows, width), 1)
    qpos = past + (row0 + lax.broadcasted_iota(I32, (rows, width), 0)) % n_t
    causal = kpos <= qpos
    key_scr[:, :past] = _mask_scores(score_ref[...], causal[:, :past])
    key_scr[:, past:] = _mask_scores(snew_ref[...], causal[:, past:])

    def count(pred):
        return jnp.sum(jnp.where(pred(key_scr[...], kpos), 1.0, 0.0), axis=1, keepdims=True)

    thr, pcut = _topk_threshold(count, (rows, 1), topk, pcut_scr, int(width - 1).bit_length())
    thr_ref[...] = thr
    pcut_ref[...] = pcut


def _thr_sample(score2d, snew2d, n_t, past, topk, rb):
    m = score2d.shape[0]
    width = past + LANES
    row = lambda i: (i, 0)
    return pl.pallas_call(
        functools.partial(_thr_sample_kernel, n_t=n_t, past=past, topk=topk),
        grid=(m // rb,),
        in_specs=[pl.BlockSpec((rb, past), row), pl.BlockSpec((rb, LANES), row)],
        out_specs=[pl.BlockSpec((rb, 1), row), pl.BlockSpec((rb, 1), row)],
        out_shape=[jax.ShapeDtypeStruct((m, 1), F32), jax.ShapeDtypeStruct((m, 1), I32)],
        scratch_shapes=[pltpu.VMEM((rb, width), F32), pltpu.VMEM((rb, 1), I32)],
        compiler_params=_cparams(("parallel",)),
        name="thr_sample",
    )(score2d, snew2d)


def _attn_sample_kernel(pt_ref, q_ref, score_ref, snew_ref, thr_ref, pcut_ref, knew_ref, vnew_ref,
                        k_hbm, v_hbm, o_ref, m_scr, l_scr, acc_scr, k_buf, v_buf, sems,
                        *, n_pg, n_t, past, n_new, topk):
    slot = _fetch_pages(pt_ref, [k_hbm, v_hbm], [k_buf, v_buf], sems, n_pg)
    g = pl.program_id(1)
    rows = n_t * DSA_HEADS
    width = n_pg * PAGE_SIZE

    @pl.when(g == 0)
    def _():
        m_scr[...] = jnp.full(m_scr.shape, -jnp.inf, F32)
        l_scr[...] = jnp.zeros(l_scr.shape, F32)
        acc_scr[...] = jnp.zeros(acc_scr.shape, F32)

    thr = thr_ref[0]
    pcut = pcut_ref[0]
    qpos = past + lax.broadcasted_iota(I32, (n_t, 1), 0)

    def per_head(a):
        return jnp.broadcast_to(a[:, None, :], (n_t, DSA_HEADS, a.shape[1])).reshape(rows, a.shape[1])

    def bias_of(score, kpos):
        sel = _select(_mask_scores(score, kpos <= qpos), thr, pcut, kpos, qpos, topk)
        return per_head(jnp.where(sel, 0.0, -jnp.inf))

    head_of_row = lax.broadcasted_iota(I32, (rows, DSA_W), 0) % DSA_HEADS
    head_of_col = lax.broadcasted_iota(I32, (rows, DSA_W), 1) // DSA_HEAD_DIM
    diag = head_of_row == head_of_col
    qbd = jnp.where(diag, per_head(q_ref[0]), 0.0).astype(BF16)

    def update(logits, pv):
        m_old = m_scr[...]
        m_new = jnp.maximum(m_old, jnp.max(logits, axis=1, keepdims=True))
        m_safe = jnp.where(m_new == -jnp.inf, 0.0, m_new)
        e = jnp.exp(logits - m_safe)
        alpha = jnp.exp(m_old - m_safe)
        l_scr[...] = alpha * l_scr[...] + jnp.sum(e, axis=1, keepdims=True)
        acc_scr[...] = alpha * acc_scr[...] + pv(e.astype(BF16))
        m_scr[...] = m_new

    def page(buf, i):
        return buf[slot, i].reshape(DSA_W, PAGE_SIZE).astype(BF16)

    def pv_pages(e):
        acc = _dot_nt(e[:, :PAGE_SIZE], page(v_buf, 0))
        for i in range(1, n_pg):
            acc = acc + _dot_nt(e[:, i * PAGE_SIZE:(i + 1) * PAGE_SIZE], page(v_buf, i))
        return acc

    kpos = g * width + lax.broadcasted_iota(I32, (n_t, width), 1)
    logits = jnp.concatenate([_dot(qbd, page(k_buf, i)) for i in range(n_pg)], axis=1)
    update(logits + bias_of(score_ref[0], kpos), pv_pages)

    @pl.when(g == pl.num_programs(1) - 1)
    def _():
        kpos_new = past + lax.broadcasted_iota(I32, (n_t, n_new), 1)
        bias_new = bias_of(snew_ref[0][:, :n_new], kpos_new)
        update(_dot_nt(qbd, knew_ref[0]) + bias_new, lambda e: _dot(e, vnew_ref[0]))
        o = jnp.where(diag, acc_scr[...] / l_scr[...], 0.0)
        o_ref[0] = jnp.sum(o.reshape(n_t, DSA_HEADS, DSA_W), axis=1)


def _attn_sample(page_table_flat, q, score, snew, thr, pcut, knew, vnew, k_pages, v_pages,
                 n_t, n_pages, n_pg, past, topk):
    bsz = q.shape[0]
    ngrp = n_pages // n_pg
    n_new = knew.shape[1]
    rows = n_t * DSA_HEADS
    per_b = lambda b, g, pt: (b, 0, 0)
    page_shape = (DSA_HEADS, DSA_HEAD_DIM, PAGE_SIZE)
    grid_spec = pltpu.PrefetchScalarGridSpec(
        num_scalar_prefetch=1,
        grid=(bsz, ngrp),
        in_specs=[
            pl.BlockSpec((1, n_t, DSA_W), per_b),
            pl.BlockSpec((1, n_t, n_pg * PAGE_SIZE), lambda b, g, pt: (b, 0, g)),
            pl.BlockSpec((1, n_t, LANES), per_b),
            pl.BlockSpec((1, n_t, 1), per_b),
            pl.BlockSpec((1, n_t, 1), per_b),
            pl.BlockSpec((1, n_new, DSA_W), per_b),
            pl.BlockSpec((1, n_new, DSA_W), per_b),
            pl.BlockSpec(memory_space=pl.ANY),
            pl.BlockSpec(memory_space=pl.ANY),
        ],
        out_specs=pl.BlockSpec((1, n_t, DSA_W), per_b),
        scratch_shapes=[
            pltpu.VMEM((rows, 1), F32),
            pltpu.VMEM((rows, 1), F32),
            pltpu.VMEM((rows, DSA_W), F32),
            pltpu.VMEM((2, n_pg) + page_shape, F32),
            pltpu.VMEM((2, n_pg) + page_shape, F32),
            pltpu.SemaphoreType.DMA((2, 2)),
        ],
    )
    return pl.pallas_call(
        functools.partial(_attn_sample_kernel, n_pg=n_pg, n_t=n_t, past=past, n_new=n_new, topk=topk),
        grid_spec=grid_spec,
        out_shape=jax.ShapeDtypeStruct((bsz, n_t, DSA_W), F32),
        compiler_params=_cparams(("arbitrary", "arbitrary")),
        name="attn_sample",
    )(page_table_flat, q, score, snew, thr, pcut, knew, vnew, k_pages, v_pages)


def _layer_norm(x, g, b):
    mu = jnp.mean(x, axis=-1, keepdims=True)
    xc = x - mu
    var = jnp.mean(xc * xc, axis=-1, keepdims=True)
    return xc * lax.rsqrt(var + NORM_EPS) * g + b


def _post_kernel(x_ref, ga_ref, ds_ref, wo_ref, l1g_ref, l1b_ref, wg_ref, wu_ref, wd_ref,
                 l2g_ref, l2b_ref, y_ref, *, alpha):
    mix = _dot(ga_ref[...], wo_ref[:GV_W, :]) + _dot(ds_ref[...], wo_ref[GV_W:, :])
    h = _layer_norm(alpha * x_ref[...] + mix, l1g_ref[...], l1b_ref[...])
    hb = h.astype(BF16)
    a = _dot(hb, wg_ref[...])
    u = _dot(hb, wu_ref[...])
    f = _dot((a * jax.nn.sigmoid(a) * u).astype(BF16), wd_ref[...])
    y_ref[...] = _layer_norm(alpha * h + f, l2g_ref[...], l2b_ref[...])


def _post(x2d, gated, dsa, wo, l1g, l1b, wg, wu, wd, l2g, l2b, alpha, tm):
    m, d = x2d.shape
    dff = wg.shape[1]
    tm = min(tm, m)
    assert m % tm == 0
    row = lambda i: (i, 0)
    return pl.pallas_call(
        functools.partial(_post_kernel, alpha=alpha),
        grid=(m // tm,),
        in_specs=[
            pl.BlockSpec((tm, d), row),
            pl.BlockSpec((tm, GV_W), row),
            pl.BlockSpec((tm, DSA_W), row),
            _const_spec((GV_W + DSA_W, d)),
            _const_spec((1, d)),
            _const_spec((1, d)),
            _const_spec((d, dff)),
            _const_spec((d, dff)),
            _const_spec((dff, d)),
            _const_spec((1, d)),
            _const_spec((1, d)),
        ],
        out_specs=pl.BlockSpec((tm, d), row),
        out_shape=jax.ShapeDtypeStruct((m, d), F32),
        compiler_params=_cparams(("parallel",)),
        name="post",
    )(x2d, gated, dsa, wo, l1g, l1b, wg, wu, wd, l2g, l2b)


def _layer_weights(l, w_in, w_lr_up, b_gate, gla_norm_g, w_o, ln1_g, ln1_b,
                   w_ffn_gate, w_ffn_up, w_ffn_down, ln2_g, ln2_b):
    row = lambda a: a[l][None, :].astype(F32)
    w, wvt = _prep_w_in(w_in[l])
    return dict(
        w=w, wvt=wvt, wlr=_prep_w_lr(w_lr_up[l]), bg=row(b_gate), gn=row(gla_norm_g),
        wo=w_o[l].astype(BF16), l1g=row(ln1_g), l1b=row(ln1_b),
        wg=w_ffn_gate[l].astype(BF16), wu=w_ffn_up[l].astype(BF16), wd=w_ffn_down[l].astype(BF16),
        l2g=row(ln2_g), l2b=row(ln2_b))


def _prompt_layer(x, wts, alpha):
    bsz, length, d = x.shape
    tm = 512 if length % 512 == 0 else length
    tabs = _rope_tables(jnp.arange(length, dtype=F32))
    x2d = x.reshape(bsz * length, d)
    (gq, gk, gv, gr, gg, dq, dkb, iq, ikb, kt, vt, vtb, ikt, iwt) = _project_fm(
        x2d, wts["w"], wts["wvt"], wts["wlr"], wts["bg"], *tabs, tm, bsz)
    r3 = lambda a: a.reshape(bsz, length, a.shape[-1])
    chunk = int(np.gcd(length, GLA_CHUNK))
    n_sub = 4 if length % (4 * chunk) == 0 else 1
    s0 = jnp.zeros((bsz, GLA_HEADS, GLA_DK, GLA_DV), F32)
    gated, state = _gla(r3(gq), r3(gk), r3(gg), r3(gv), r3(gr), s0, wts["gn"], chunk, n_sub,
                        4 if bsz % 4 == 0 else 1)
    kt_tile = 256 if length % 256 == 0 else 128
    dsa = _dsa_prompt(r3(dq), r3(iq), iwt, r3(dkb), vtb, r3(ikb), 128, kt_tile)
    y = _post(x2d, gated.reshape(bsz * length, GV_W), dsa.reshape(bsz * length, DSA_W),
              wts["wo"], wts["l1g"], wts["l1b"], wts["wg"], wts["wu"], wts["wd"],
              wts["l2g"], wts["l2b"], alpha, 256)
    heads = lambda a: jnp.transpose(a.reshape(bsz, DSA_HEADS, DSA_HEAD_DIM, length), (0, 3, 1, 2))
    return (y.reshape(bsz, length, d), heads(kt), heads(vt), jnp.transpose(ikt, (0, 2, 1)), state)


def _sample_layer(x, wts, alpha, cache_k, cache_v, cache_kidx, state, page_table):
    bsz, n_t, d = x.shape
    n_pages = page_table.shape[1]
    past = n_pages * PAGE_SIZE
    m = bsz * n_t
    pos = past + jnp.arange(n_t, dtype=F32)
    tabs = [jnp.tile(t, (bsz, 1)) for t in _rope_tables(pos)]
    x2d = x.reshape(m, d)
    (gq, gk, gv, gr, gg, dq, dkb, iq, ikb, dk, dv, dvb, ik, iw) = _project(
        x2d, wts["w"], wts["wlr"], wts["bg"], *tabs, m)
    chunk = 16
    n_seq = 8 if bsz % 8 == 0 else 1
    pad3 = lambda a: jnp.pad(a.reshape(bsz, n_t, a.shape[-1]), ((0, 0), (0, chunk - n_t), (0, 0)))
    gated, state_new = _gla(pad3(gq), pad3(gk), pad3(gg), pad3(gv), pad3(gr), state, wts["gn"],
                            chunk, 1, n_seq)
    gated = gated[:, :n_t].reshape(m, GV_W)
    topk = min(IDX_TOPK, (past + n_t) // 4)
    pick = lambda want: max(c for c in (1, 2, 4, 8, 16, 32, 64) if c <= want and n_pages % c == 0)
    pt_flat = page_table.reshape(-1)
    q_rows = iq.reshape(bsz, n_t * IDX_HEADS, IDX_DIM)
    w_rows = iw.reshape(bsz, n_t * IDX_HEADS, 1)
    ik_new = jnp.pad(ikb[:, :IDX_DIM].reshape(bsz, n_t, IDX_DIM), ((0, 0), (0, LANES - n_t), (0, 0)))
    score, snew = _idx_sample(pt_flat, q_rows, w_rows, ik_new, cache_kidx, n_t, n_pages, pick(64))
    thr, pcut = _thr_sample(score.reshape(m, past), snew.reshape(m, LANES), n_t, past, topk,
                            64 if m % 64 == 0 else 32)
    n_new = 16
    new3 = lambda a: jnp.pad(a.reshape(bsz, n_t, DSA_W), ((0, 0), (0, n_new - n_t), (0, 0)))
    dsa = _attn_sample(pt_flat, dq.astype(F32).reshape(bsz, n_t, DSA_W), score, snew,
                       thr.reshape(bsz, n_t, 1), pcut.reshape(bsz, n_t, 1), new3(dkb), new3(dvb),
                       cache_k, cache_v, n_t, n_pages, pick(16), past, topk)
    y = _post(x2d, gated, dsa.reshape(m, DSA_W).astype(BF16),
              wts["wo"], wts["l1g"], wts["l1b"], wts["wg"], wts["wu"], wts["wd"],
              wts["l2g"], wts["l2b"], alpha, 256)
    return (y.reshape(bsz, n_t, d),
            dk.reshape(bsz, n_t, DSA_HEADS, DSA_HEAD_DIM),
            dv.reshape(bsz, n_t, DSA_HEADS, DSA_HEAD_DIM),
            ik.reshape(bsz, n_t, IDX_DIM), state_new)


def kernel(x_prompt, x_sample, cache_k, cache_v, cache_kidx, state_gla, page_table, w_in, w_lr_up, b_gate, gla_norm_g, w_o, ln1_g, ln1_b, w_ffn_gate, w_ffn_up, w_ffn_down, ln2_g, ln2_b):
    depth = w_in.shape[0]
    alpha = float((2 * depth) ** 0.25)
    n_pool = cache_k.shape[1]
    xp, xs = x_prompt, x_sample
    outs_p = [[] for _ in range(4)]
    outs_s = [[] for _ in range(4)]
    for l in range(depth):
        wts = _layer_weights(l, w_in, w_lr_up, b_gate, gla_norm_g, w_o, ln1_g, ln1_b,
                             w_ffn_gate, w_ffn_up, w_ffn_down, ln2_g, ln2_b)
        xp, *rest_p = _prompt_layer(xp, wts, alpha)
        feat = lambda c: jnp.transpose(c, (0, 1, 3, 4, 2)).reshape(
            depth * n_pool, DSA_HEADS, DSA_HEAD_DIM, PAGE_SIZE)
        xs, *rest_s = _sample_layer(
            xs, wts, alpha, feat(cache_k), feat(cache_v),
            jnp.transpose(cache_kidx, (0, 1, 3, 2)).reshape(depth * n_pool, IDX_DIM, PAGE_SIZE),
            state_gla[l], page_table + l * n_pool)
        for acc, val in zip(outs_p, rest_p):
            acc.append(val)
        for acc, val in zip(outs_s, rest_s):
            acc.append(val)
    stack = lambda seq: jnp.stack(seq)
    return (xp, xs, *[stack(a) for a in outs_p], *[stack(a) for a in outs_s])
```
